```python
import jax, jax.numpy as jnp
from jax import lax
import numpy as np

D_MODEL = 2048
BATCH = 16
SEQ = 2048
DEPTH = 1

MEM_LEN = 256
NORM_EPS = 1e-6

LRU_WIDTH = D_MODEL
LRU_HEADS = 16
LRU_BLOCK = LRU_WIDTH // LRU_HEADS
CONV_WIDTH = 4
LRU_C = 8.0

SB_HEADS = 16
SB_HEAD_DIM = 64
SB_WIDTH = SB_HEADS * SB_HEAD_DIM
Q_BLOCK = 128

MEM_HEADS = 4
MEM_HEAD_DIM = 256
MEM_WIDTH = MEM_HEADS * MEM_HEAD_DIM

N_BRANCH = 3
D_FF = -(-8 * D_MODEL // (3 * 256)) * 256

IN_COLS = 2 * LRU_WIDTH + 3 * SB_WIDTH + MEM_WIDTH + N_BRANCH * D_MODEL
IN_SPLITS = [LRU_WIDTH,
             2 * LRU_WIDTH,
             2 * LRU_WIDTH + SB_WIDTH,
             2 * LRU_WIDTH + 2 * SB_WIDTH,
             2 * LRU_WIDTH + 3 * SB_WIDTH,
             2 * LRU_WIDTH + 3 * SB_WIDTH + MEM_WIDTH]

kernel_name = 'hybrid_rglru_stickbreak_memxattn_swiglu'


def rms_norm(x, g):
    xf = x.astype(jnp.float32)
    y = xf * lax.rsqrt(jnp.mean(xf * xf, axis=-1, keepdims=True) + NORM_EPS)
    return (y * g.astype(jnp.float32)).astype(x.dtype)


def rg_lru_block(u, gate_in, conv_w, conv_b, wa, ba, wx, bx, lam):
    B, T, W = u.shape
    c = lax.conv_general_dilated(u, conv_w[:, None, :].astype(u.dtype), window_strides=(1,),
                                 padding=[(CONV_WIDTH - 1, 0)],
                                 dimension_numbers=('NWC', 'WIO', 'NWC'),
                                 feature_group_count=W) + conv_b
    cb = c.reshape(B, T, LRU_HEADS, LRU_BLOCK)
    r = jax.nn.sigmoid(jnp.einsum('bthi,hij->bthj', cb, wa) + ba).reshape(B, T, W)
    i = jax.nn.sigmoid(jnp.einsum('bthi,hij->bthj', cb, wx) + bx).reshape(B, T, W)
    log_a = -LRU_C * r.astype(jnp.float32) * jax.nn.softplus(-lam.astype(jnp.float32))
    a = jnp.exp(log_a)
    mult = jnp.sqrt(jnp.maximum(-jnp.expm1(2.0 * log_a), 0.0))
    is_start = (jnp.arange(T) == 0)[None, :, None]
    mult = jnp.where(is_start, 1.0, mult)
    b_in = mult * (i * c).astype(jnp.float32)

    def step(h, inp):
        a_t, b_t = inp
        h = a_t * h + b_t
        return h, h

    h0 = jnp.zeros((B, W), jnp.float32)
    _, hs = lax.scan(step, h0, (jnp.swapaxes(a, 0, 1), jnp.swapaxes(b_in, 0, 1)))
    y = jnp.swapaxes(hs, 0, 1).astype(u.dtype)
    return y * jax.nn.gelu(gate_in)


def stick_breaking_attention(q, k, v):
    B, T, H, Dh = q.shape
    scale = Dh ** -0.5
    outs = []
    for blk in range(T // Q_BLOCK):
        q0 = blk * Q_BLOCK
        kv_len = q0 + Q_BLOCK
        qb = q[:, q0:kv_len]
        kb = k[:, :kv_len]
        vb = v[:, :kv_len]
        z = jnp.einsum('bqhd,bkhd->bhqk', qb, kb).astype(jnp.float32) * scale
        t_idx = q0 + jnp.arange(Q_BLOCK)[:, None]
        s_idx = jnp.arange(kv_len)[None, :]
        causal = s_idx < t_idx
        log_beta = jax.nn.log_sigmoid(z)
        log_1m_beta = jnp.where(causal, jax.nn.log_sigmoid(-z), 0.0)
        tail = lax.cumsum(log_1m_beta, axis=3, reverse=True) - log_1m_beta
        w = jnp.where(causal, jnp.exp(log_beta + tail), 0.0)
        outs.append(jnp.einsum('bhqk,bkhd->bqhd', w.astype(vb.dtype), vb))
    return jnp.concatenate(outs, axis=1)


def memory_cross_attention(q, k, v):
    s = jnp.einsum('bthd,bmhd->bhtm', q, k).astype(jnp.float32) * (q.shape[-1] ** -0.5)
    p = jax.nn.softmax(s, axis=-1).astype(v.dtype)
    return jnp.einsum('bhtm,bmhd->bthd', p, v)


def _normal(key, shape, fan_in):
    return jax.random.normal(key, shape, jnp.float32) * (fan_in ** -0.5)


def setup_inputs(seed: int = 0) -> dict:
    key = jax.random.key(seed)
    ks = jax.random.split(key, 26)
    L = DEPTH

    def gain(k, shape):
        return 1.0 + 0.02 * jax.random.normal(k, shape, jnp.float32)

    def bias(k, shape):
        return 0.01 * jax.random.normal(k, shape, jnp.float32)

    u = jax.random.uniform(ks[10], (L, LRU_WIDTH), jnp.float32, 0.9, 0.999)
    a0 = u ** (1.0 / LRU_C)
    lru_lambda = jnp.log(a0) - jnp.log1p(-a0)
    return {
        'x': jax.random.normal(ks[0], (BATCH, SEQ, D_MODEL), jnp.float32),
        'mem': jax.random.normal(ks[1], (BATCH, MEM_LEN, D_MODEL), jnp.float32),
        'g_mix': gain(ks[2], (L, D_MODEL)),
        'g_mem': gain(ks[3], (L, D_MODEL)),
        'w_in': _normal(ks[4], (L, D_MODEL, IN_COLS), D_MODEL),
        'b_gate': bias(ks[5], (L, N_BRANCH * D_MODEL)),
        'conv_w': _normal(ks[6], (L, CONV_WIDTH, LRU_WIDTH), CONV_WIDTH),
        'conv_b': bias(ks[7], (L, LRU_WIDTH)),
        'lru_wa': _normal(ks[8], (L, LRU_HEADS, LRU_BLOCK, LRU_BLOCK), LRU_BLOCK),
        'lru_ba': bias(ks[9], (L, LRU_HEADS, LRU_BLOCK)),
        'lru_wx': _normal(ks[11], (L, LRU_HEADS, LRU_BLOCK, LRU_BLOCK), LRU_BLOCK),
        'lru_bx': bias(ks[12], (L, LRU_HEADS, LRU_BLOCK)),
        'lru_lambda': lru_lambda,
        'sb_gq': gain(ks[13], (L, SB_HEAD_DIM)),
        'sb_gk': gain(ks[14], (L, SB_HEAD_DIM)),
        'mem_w_kv': _normal(ks[15], (L, D_MODEL, 2 * MEM_WIDTH), D_MODEL),
        'mem_gq': gain(ks[16], (L, MEM_HEAD_DIM)),
        'mem_gk': gain(ks[17], (L, MEM_HEAD_DIM)),
        'w_pa': _normal(ks[18], (L, LRU_WIDTH, D_MODEL), LRU_WIDTH),
        'w_pb': _normal(ks[19], (L, SB_WIDTH, D_MODEL), SB_WIDTH),
        'w_pc': _normal(ks[20], (L, MEM_WIDTH, D_MODEL), MEM_WIDTH),
        'w_out': _normal(ks[21], (L, D_MODEL, D_MODEL), D_MODEL),
        'g_ffn': gain(ks[22], (L, D_MODEL)),
        'w_fc': _normal(ks[23], (L, D_MODEL, 2 * D_FF), D_MODEL),
        'w_down': _normal(ks[24], (L, D_FF, D_MODEL), D_FF),
    }


def reference(x, mem, g_mix, g_mem, w_in, b_gate, conv_w, conv_b, lru_wa, lru_ba, lru_wx, lru_bx,
              lru_lambda, sb_gq, sb_gk, mem_w_kv, mem_gq, mem_gk, w_pa, w_pb, w_pc, w_out,
              g_ffn, w_fc, w_down):
    B, T, D = x.shape
    M = mem.shape[1]
    for l in range(DEPTH):
        h = rms_norm(x, g_mix[l])
        proj = h @ w_in[l]
        u_lru, u_gate, q_sb, k_sb, v_sb, q_mem, gate_logits = jnp.split(proj, IN_SPLITS, axis=-1)

        y_a = rg_lru_block(u_lru, u_gate, conv_w[l], conv_b[l], lru_wa[l], lru_ba[l],
                           lru_wx[l], lru_bx[l], lru_lambda[l])

        q_sb = rms_norm(q_sb.reshape(B, T, SB_HEADS, SB_HEAD_DIM), sb_gq[l])
        k_sb = rms_norm(k_sb.reshape(B, T, SB_HEADS, SB_HEAD_DIM), sb_gk[l])
        v_sb = v_sb.reshape(B, T, SB_HEADS, SB_HEAD_DIM)
        y_b = stick_breaking_attention(q_sb, k_sb, v_sb).reshape(B, T, SB_WIDTH)

        m = rms_norm(mem, g_mem[l])
        k_m, v_m = jnp.split(m @ mem_w_kv[l], 2, axis=-1)
        k_m = rms_norm(k_m.reshape(B, M, MEM_HEADS, MEM_HEAD_DIM), mem_gk[l])
        v_m = v_m.reshape(B, M, MEM_HEADS, MEM_HEAD_DIM)
        q_m = rms_norm(q_mem.reshape(B, T, MEM_HEADS, MEM_HEAD_DIM), mem_gq[l])
        y_c = memory_cross_attention(q_m, k_m, v_m).reshape(B, T, MEM_WIDTH)

        gates = jax.nn.sigmoid(gate_logits + b_gate[l]).reshape(B, T, N_BRANCH, D)
        merged = (gates[:, :, 0] * (y_a @ w_pa[l])
                  + gates[:, :, 1] * (y_b @ w_pb[l])
                  + gates[:, :, 2] * (y_c @ w_pc[l]))
        x = x + merged @ w_out[l]

        h2 = rms_norm(x, g_ffn[l])
        f_gate, f_up = jnp.split(h2 @ w_fc[l], 2, axis=-1)
        x = x + (jax.nn.silu(f_gate) * f_up) @ w_down[l]
    return x
```

```python
import functools

import jax
import jax.numpy as jnp
from jax import lax
from jax.experimental import pallas as pl
from jax.experimental.pallas import tpu as pltpu

F32 = jnp.float32
BF16 = jnp.bfloat16

NORM_EPS = 1e-6
LRU_C = 8.0
CONV_WIDTH = 4
LRU_BLOCK = 128
SB_HEAD_DIM = 64
MEM_HEAD_DIM = 256
MEM_HEADS = 4

LANES = 128
SUBLANES = 8
MIB = 1024 * 1024

EXP_ZERO_BELOW = -104.0


def _params(semantics, vmem_mib):
    return pltpu.CompilerParams(dimension_semantics=semantics, vmem_limit_bytes=vmem_mib * MIB)


NORM_CHUNK = 64


def _rms_rows(x, g):
    ms = jnp.mean(x * x, axis=-1, keepdims=True)
    return x * lax.rsqrt(ms + NORM_EPS) * g


def _norm_matmul_kernel(x_ref, g_ref, w_ref, o_ref, h_ref):
    @pl.when(pl.program_id(1) == 0)
    def _():
        g = g_ref[...]

        def body(c, carry):
            rows = pl.ds(pl.multiple_of(c * NORM_CHUNK, NORM_CHUNK), NORM_CHUNK)
            h_ref[rows, :] = _rms_rows(x_ref[rows, :].astype(F32), g).astype(BF16)
            return carry

        lax.fori_loop(0, x_ref.shape[0] // NORM_CHUNK, body, 0)

    o_ref[...] = jnp.dot(h_ref[...], w_ref[...], preferred_element_type=F32).astype(o_ref.dtype)


def _norm_matmul(x2d, g, w, tm, tn, vmem_mib, name):
    m, k = x2d.shape
    n = w.shape[1]
    return pl.pallas_call(
        _norm_matmul_kernel,
        out_shape=jax.ShapeDtypeStruct((m, n), BF16),
        grid=(m // tm, n // tn),
        in_specs=[
            pl.BlockSpec((tm, k), lambda i, j: (i, 0)),
            pl.BlockSpec((1, k), lambda i, j: (0, 0)),
            pl.BlockSpec((k, tn), lambda i, j: (0, j)),
        ],
        out_specs=pl.BlockSpec((tm, tn), lambda i, j: (i, j)),
        scratch_shapes=[pltpu.VMEM((tm, k), BF16)],
        compiler_params=_params(("arbitrary", "arbitrary"), vmem_mib),
        name=name,
    )(x2d, g.reshape(1, k), w)


LRU_CHUNK = 256


def _gelu_tanh(x):
    return 0.5 * x * (1.0 + jnp.tanh(0.7978845608028654 * (x + 0.044715 * x * x * x)))


def _lru_kernel(u_ref, gate_ref, cw_ref, cb_ref, w_ref, b_ref, lam_ref, o_ref, c_s, a_s, b_s):
    t_len = u_ref.shape[1]
    u = u_ref[0].astype(F32)
    row = lax.broadcasted_iota(jnp.int32, (t_len, 1), 0)
    c = u * cw_ref[CONV_WIDTH - 1:CONV_WIDTH, :]
    for s in range(1, CONV_WIDTH):
        shifted = jnp.where(row >= s, pltpu.roll(u, s, 0), 0.0)
        c = c + shifted * cw_ref[CONV_WIDTH - 1 - s:CONV_WIDTH - s, :]
    c_s[...] = c + cb_ref[...]

    lam = lam_ref[...]
    neg_sp = -LRU_C * (jnp.maximum(-lam, 0.0) + jnp.log1p(jnp.exp(-jnp.abs(lam))))
    sub = lax.broadcasted_iota(jnp.int32, (LRU_CHUNK, 1), 0) % SUBLANES

    def gates(ci, carry):
        rows = pl.ds(pl.multiple_of(ci * LRU_CHUNK, LRU_CHUNK), LRU_CHUNK)
        cc = c_s[rows, :]
        g = jnp.dot(cc.astype(BF16), w_ref[0], preferred_element_type=F32) + b_ref[0]
        r = jax.nn.sigmoid(g[:, :LRU_BLOCK])
        i = jax.nn.sigmoid(g[:, LRU_BLOCK:])
        log_a = r * neg_sp
        a = jnp.exp(log_a)
        mult = jnp.sqrt(jnp.maximum(-jnp.tanh(log_a) * (a * a + 1.0), 0.0))
        t_idx = ci * LRU_CHUNK + lax.broadcasted_iota(jnp.int32, (LRU_CHUNK, 1), 0)
        mult = jnp.where(t_idx == 0, 1.0, mult)
        b = mult * (i * cc)
        for k in (1, 2, 4):
            keep = sub >= k
            a_prev = jnp.where(keep, pltpu.roll(a, k, 0), 1.0)
            b_prev = jnp.where(keep, pltpu.roll(b, k, 0), 0.0)
            b = a * b_prev + b
            a = a * a_prev
        a_s[rows, :] = a
        b_s[rows, :] = b
        return carry

    lax.fori_loop(0, t_len // LRU_CHUNK, gates, 0)

    def chain(gi, h_last):
        rows = pl.ds(pl.multiple_of(gi * SUBLANES, SUBLANES), SUBLANES)
        h = b_s[rows, :] + a_s[rows, :] * h_last
        b_s[rows, :] = h
        return jnp.broadcast_to(h[SUBLANES - 1:SUBLANES, :], h.shape)

    lax.fori_loop(0, t_len // SUBLANES, chain, jnp.zeros((SUBLANES, LRU_BLOCK), F32), unroll=8)

    def emit(ci, carry):
        rows = pl.ds(pl.multiple_of(ci * LRU_CHUNK, LRU_CHUNK), LRU_CHUNK)
        gate = gate_ref[0, rows, :].astype(F32)
        o_ref[0, rows, :] = (b_s[rows, :] * _gelu_tanh(gate)).astype(o_ref.dtype)
        return carry

    lax.fori_loop(0, t_len // LRU_CHUNK, emit, 0)


def _lru(proj3, conv_w, conv_b, w_cat, b_cat, lam, width):
    bsz, t_len, _ = proj3.shape
    heads = width // LRU_BLOCK
    return pl.pallas_call(
        _lru_kernel,
        out_shape=jax.ShapeDtypeStruct((bsz, t_len, width), BF16),
        grid=(bsz, heads),
        in_specs=[
            pl.BlockSpec((1, t_len, LRU_BLOCK), lambda b, h: (b, 0, h)),
            pl.BlockSpec((1, t_len, LRU_BLOCK), lambda b, h: (b, 0, heads + h)),
            pl.BlockSpec((CONV_WIDTH, LRU_BLOCK), lambda b, h: (0, h)),
            pl.BlockSpec((1, LRU_BLOCK), lambda b, h: (0, h)),
            pl.BlockSpec((1, LRU_BLOCK, 2 * LRU_BLOCK), lambda b, h: (h, 0, 0)),
            pl.BlockSpec((1, 1, 2 * LRU_BLOCK), lambda b, h: (h, 0, 0)),
            pl.BlockSpec((1, LRU_BLOCK), lambda b, h: (0, h)),
        ],
        out_specs=pl.BlockSpec((1, t_len, LRU_BLOCK), lambda b, h: (b, 0, h)),
        scratch_shapes=[pltpu.VMEM((t_len, LRU_BLOCK), F32)] * 3,
        compiler_params=_params(("arbitrary", "arbitrary"), 32),
        name="lru",
    )(proj3, proj3, conv_w, conv_b.reshape(1, width), w_cat, b_cat, lam.reshape(1, width))


SB_TILE = 256


def _pair_rms(x, g, first_head):
    x2 = x * x
    s0 = jnp.sum(jnp.where(first_head, x2, 0.0), axis=-1, keepdims=True)
    s1 = jnp.sum(jnp.where(first_head, 0.0, x2), axis=-1, keepdims=True)
    rs = jnp.where(first_head, lax.rsqrt(s0 / SB_HEAD_DIM + NORM_EPS), lax.rsqrt(s1 / SB_HEAD_DIM + NORM_EPS))
    return x * rs * g


def _sb_kernel(q_ref, k_ref, v_ref, gq_ref, gk_ref, tri_ref, o_ref, kn_s):
    t_len = k_ref.shape[1]
    qi = pl.program_id(2)
    first_head = lax.broadcasted_iota(jnp.int32, (1, LANES), 1) < SB_HEAD_DIM

    @pl.when(qi == 0)
    def _():
        gk = gk_ref[...]

        def body(c, carry):
            rows = pl.ds(pl.multiple_of(c * SB_TILE, SB_TILE), SB_TILE)
            kn_s[rows, :] = _pair_rms(k_ref[0, rows, :].astype(F32), gk, first_head).astype(BF16)
            return carry

        lax.fori_loop(0, t_len // SB_TILE, body, 0)

    scale = SB_HEAD_DIM ** -0.5
    qs = _pair_rms(q_ref[0].astype(F32), gq_ref[...], first_head).astype(BF16) * jnp.asarray(scale, BF16)
    zero = jnp.zeros((), BF16)
    tri = tri_ref[...]
    r_idx = lax.broadcasted_iota(jnp.int32, (SB_TILE, SB_TILE), 0)
    c_idx = lax.broadcasted_iota(jnp.int32, (SB_TILE, SB_TILE), 1)
    causal = c_idx < r_idx

    def block(kb, masked, acc, carry0, carry1):
        rows = pl.ds(pl.multiple_of(kb * SB_TILE, SB_TILE), SB_TILE)
        kblk = kn_s[rows, :]
        vblk = v_ref[0, rows, :]
        k2 = jnp.concatenate([jnp.where(first_head, kblk, zero), jnp.where(first_head, zero, kblk)], axis=0)
        v2 = jnp.concatenate([jnp.where(first_head, vblk, zero), jnp.where(first_head, zero, vblk)], axis=0)
        z = lax.dot_general(qs, k2, (((1,), (1,)), ((), ())), preferred_element_type=F32)
        ws = []
        sums = []
        for h in range(2):
            zh = z[:, h * SB_TILE:(h + 1) * SB_TILE]
            ls = jnp.minimum(zh, 0.0) - jnp.log1p(jnp.exp(-jnp.abs(zh)))
            l1m = ls - zh
            if masked:
                l1m = jnp.where(causal, l1m, 0.0)
            hi = l1m.astype(BF16)
            lo = (l1m - hi.astype(F32)).astype(BF16)
            tail = jnp.dot(jnp.concatenate([hi, lo], axis=1), tri, preferred_element_type=F32)
            w = jnp.exp(ls + tail)
            if masked:
                w = jnp.where(causal, w, 0.0)
            ws.append(w.astype(BF16))
            sums.append(jnp.sum(l1m, axis=-1, keepdims=True))
        pv = jnp.dot(jnp.concatenate(ws, axis=1), v2, preferred_element_type=F32)
        later = jnp.where(first_head, jnp.exp(carry0), jnp.exp(carry1))
        return acc + later * pv, carry0 + sums[0], carry1 + sums[1]

    zcol = jnp.zeros((SB_TILE, 1), F32)
    acc, carry0, carry1 = block(qi, True, jnp.zeros((SB_TILE, LANES), F32), zcol, zcol)

    def cond(state):
        kb, live = state[0], state[1]
        return jnp.logical_and(kb >= 0, live > 0)

    def body(state):
        kb, _, acc, carry0, carry1 = state
        acc, carry0, carry1 = block(kb, False, acc, carry0, carry1)
        live = (jnp.max(jnp.maximum(carry0, carry1)) > EXP_ZERO_BELOW).astype(jnp.int32)
        return kb - 1, live, acc, carry0, carry1

    state = lax.while_loop(cond, body, (qi - 1, jnp.int32(1), acc, carry0, carry1))
    o_ref[0] = state[2].astype(o_ref.dtype)


def _sb_attention(proj3, gq2, gk2, tri2, q_col, k_col, v_col, heads):
    bsz, t_len, _ = proj3.shape
    pairs = heads // 2
    return pl.pallas_call(
        _sb_kernel,
        out_shape=jax.ShapeDtypeStruct((bsz, t_len, heads * SB_HEAD_DIM), BF16),
        grid=(bsz, pairs, t_len // SB_TILE),
        in_specs=[
            pl.BlockSpec((1, SB_TILE, LANES), lambda b, p, i: (b, i, q_col // LANES + p)),
            pl.BlockSpec((1, t_len, LANES), lambda b, p, i: (b, 0, k_col // LANES + p)),
            pl.BlockSpec((1, t_len, LANES), lambda b, p, i: (b, 0, v_col // LANES + p)),
            pl.BlockSpec((1, LANES), lambda b, p, i: (0, 0)),
            pl.BlockSpec((1, LANES), lambda b, p, i: (0, 0)),
            pl.BlockSpec((2 * SB_TILE, SB_TILE), lambda b, p, i: (0, 0)),
        ],
        out_specs=pl.BlockSpec((1, SB_TILE, LANES), lambda b, p, i: (b, i, p)),
        scratch_shapes=[pltpu.VMEM((t_len, LANES), BF16)],
        compiler_params=_params(("arbitrary", "arbitrary", "arbitrary"), 32),
        name="sb_attn",
    )(proj3, proj3, proj3, gq2, gk2, tri2)


MEM_TILE = 512


def _mem_attn_kernel(q_ref, kv_ref, gq_ref, gk_ref, o_ref, kn_s):
    width = MEM_HEADS * MEM_HEAD_DIM

    @pl.when(pl.program_id(1) == 0)
    def _():
        for h in range(MEM_HEADS):
            cols = slice(h * MEM_HEAD_DIM, (h + 1) * MEM_HEAD_DIM)
            kn_s[:, cols] = _rms_rows(kv_ref[0, :, cols].astype(F32), gk_ref[...]).astype(BF16)

    scale = MEM_HEAD_DIM ** -0.5
    for h in range(MEM_HEADS):
        cols = slice(h * MEM_HEAD_DIM, (h + 1) * MEM_HEAD_DIM)
        qn = _rms_rows(q_ref[0, :, cols].astype(F32), gq_ref[...]).astype(BF16) * jnp.asarray(scale, BF16)
        s = lax.dot_general(qn, kn_s[:, cols], (((1,), (1,)), ((), ())), preferred_element_type=F32)
        e = jnp.exp(s - jnp.max(s, axis=-1, keepdims=True))
        denom = jnp.sum(e, axis=-1, keepdims=True)
        v = kv_ref[0, :, width + h * MEM_HEAD_DIM:width + (h + 1) * MEM_HEAD_DIM]
        o = jnp.dot(e.astype(BF16), v, preferred_element_type=F32) / denom
        o_ref[0, :, cols] = o.astype(o_ref.dtype)


def _mem_attention(proj3, memkv3, gq, gk, q_col):
    bsz, t_len, _ = proj3.shape
    m_len = memkv3.shape[1]
    width = MEM_HEADS * MEM_HEAD_DIM
    return pl.pallas_call(
        _mem_attn_kernel,
        out_shape=jax.ShapeDtypeStruct((bsz, t_len, width), BF16),
        grid=(bsz, t_len // MEM_TILE),
        in_specs=[
            pl.BlockSpec((1, MEM_TILE, width), lambda b, i: (b, i, q_col // width)),
            pl.BlockSpec((1, m_len, 2 * width), lambda b, i: (b, 0, 0)),
            pl.BlockSpec((1, MEM_HEAD_DIM), lambda b, i: (0, 0)),
            pl.BlockSpec((1, MEM_HEAD_DIM), lambda b, i: (0, 0)),
        ],
        out_specs=pl.BlockSpec((1, MEM_TILE, width), lambda b, i: (b, i, 0)),
        scratch_shapes=[pltpu.VMEM((m_len, width), BF16)],
        compiler_params=_params(("arbitrary", "arbitrary"), 32),
        name="mem_attn",
    )(proj3, memkv3, gq.reshape(1, MEM_HEAD_DIM), gk.reshape(1, MEM_HEAD_DIM))


MERGE_TM = 512
MERGE_TN = 1024


def _merge_kernel(ya_ref, yb_ref, yc_ref, wa_ref, wb_ref, wc_ref, g0_ref, g1_ref, g2_ref,
                  b0_ref, b1_ref, b2_ref, o_ref):
    out = None
    for y_ref, w_ref, g_ref, b_ref in ((ya_ref, wa_ref, g0_ref, b0_ref), (yb_ref, wb_ref, g1_ref, b1_ref),
                                       (yc_ref, wc_ref, g2_ref, b2_ref)):
        gate = jax.nn.sigmoid(g_ref[...].astype(F32) + b_ref[...])
        term = gate * jnp.dot(y_ref[...], w_ref[...], preferred_element_type=F32)
        out = term if out is None else out + term
    o_ref[...] = out.astype(o_ref.dtype)


def _merge(ya, yb, yc, w_pa, w_pb, w_pc, proj2, b_gate, gate_col, d_model):
    n = ya.shape[0]
    tm, tn = MERGE_TM, MERGE_TN
    gate_blk = gate_col // tn
    per_branch = d_model // tn

    def y_spec(width):
        return pl.BlockSpec((tm, width), lambda i, j: (i, 0))

    def w_spec(width):
        return pl.BlockSpec((width, tn), lambda i, j: (0, j))

    def g_spec(branch):
        return pl.BlockSpec((tm, tn), lambda i, j: (i, gate_blk + branch * per_branch + j))

    def b_spec(branch):
        return pl.BlockSpec((1, tn), lambda i, j: (0, branch * per_branch + j))

    b_gate2 = b_gate.reshape(1, -1)
    return pl.pallas_call(
        _merge_kernel,
        out_shape=jax.ShapeDtypeStruct((n, d_model), BF16),
        grid=(n // tm, d_model // tn),
        in_specs=[y_spec(ya.shape[1]), y_spec(yb.shape[1]), y_spec(yc.shape[1]),
                  w_spec(w_pa.shape[0]), w_spec(w_pb.shape[0]), w_spec(w_pc.shape[0]),
                  g_spec(0), g_spec(1), g_spec(2), b_spec(0), b_spec(1), b_spec(2)],
        out_specs=pl.BlockSpec((tm, tn), lambda i, j: (i, j)),
        compiler_params=_params(("arbitrary", "arbitrary"), 48),
        name="merge",
    )(ya, yb, yc, w_pa, w_pb, w_pc, proj2, proj2, proj2, b_gate2, b_gate2, b_gate2)


OUT_TM = 512


def _out_proj_kernel(x_ref, m_ref, w_ref, g_ref, x1_ref, h2_ref):
    x1_ref[...] = x_ref[...] + jnp.dot(m_ref[...], w_ref[...], preferred_element_type=F32)
    g = g_ref[...]

    def body(c, carry):
        rows = pl.ds(pl.multiple_of(c * NORM_CHUNK, NORM_CHUNK), NORM_CHUNK)
        h2_ref[rows, :] = _rms_rows(x1_ref[rows, :], g).astype(BF16)
        return carry

    lax.fori_loop(0, x_ref.shape[0] // NORM_CHUNK, body, 0)


def _out_proj(x2, merged, w_out, g_ffn):
    n, d = x2.shape
    tm = OUT_TM
    row = pl.BlockSpec((tm, d), lambda i: (i, 0))
    return pl.pallas_call(
        _out_proj_kernel,
        out_shape=(jax.ShapeDtypeStruct((n, d), F32), jax.ShapeDtypeStruct((n, d), BF16)),
        grid=(n // tm,),
        in_specs=[row, row, pl.BlockSpec((d, d), lambda i: (0, 0)), pl.BlockSpec((1, d), lambda i: (0, 0))],
        out_specs=(row, row),
        compiler_params=_params(("arbitrary",), 48),
        name="out_proj",
    )(x2, merged, w_out, g_ffn.reshape(1, d))


FFN_TM = 512
FFN_TF = 512


def _ffn_kernel(h_ref, wg_ref, wu_ref, wd_ref, x1_ref, o_ref):
    h = h_ref[...]
    gate = jnp.dot(h, wg_ref[...], preferred_element_type=F32)
    up = jnp.dot(h, wu_ref[...], preferred_element_type=F32)
    act = (gate * jax.nn.sigmoid(gate) * up).astype(BF16)
    down = jnp.dot(act, wd_ref[...], preferred_element_type=F32)

    @pl.when(pl.program_id(1) == 0)
    def _():
        o_ref[...] = x1_ref[...] + down

    @pl.when(pl.program_id(1) != 0)
    def _():
        o_ref[...] += down


def _ffn(h2, w_fc, w_down, x1):
    n, d = h2.shape
    d_ff = w_down.shape[0]
    tm, tf = FFN_TM, FFN_TF
    n_f = d_ff // tf
    row = pl.BlockSpec((tm, d), lambda i, f: (i, 0))
    return pl.pallas_call(
        _ffn_kernel,
        out_shape=jax.ShapeDtypeStruct((n, d), F32),
        grid=(n // tm, n_f),
        in_specs=[row,
                  pl.BlockSpec((d, tf), lambda i, f: (0, f)),
                  pl.BlockSpec((d, tf), lambda i, f: (0, n_f + f)),
                  pl.BlockSpec((tf, d), lambda i, f: (f, 0)),
                  row],
        out_specs=row,
        compiler_params=_params(("arbitrary", "arbitrary"), 48),
        name="ffn",
    )(h2, w_fc, w_fc, w_down, x1)


def _layer(x, mem, g_mix, g_mem, w_in, b_gate, conv_w, conv_b, lru_wa, lru_ba, lru_wx, lru_bx,
           lru_lambda, sb_gq, sb_gk, mem_w_kv, mem_gq, mem_gk, w_pa, w_pb, w_pc, w_out,
           g_ffn, w_fc, w_down):
    bsz, t_len, d = x.shape
    m_len = mem.shape[1]
    n = bsz * t_len
    lru_width = w_pa.shape[0]
    sb_width = w_pb.shape[0]
    mem_width = w_pc.shape[0]
    in_cols = w_in.shape[1]
    q_sb_col = 2 * lru_width
    k_sb_col = q_sb_col + sb_width
    v_sb_col = k_sb_col + sb_width
    q_mem_col = v_sb_col + sb_width
    gate_col = q_mem_col + mem_width

    x2 = x.reshape(n, d)
    proj2 = _norm_matmul(x2, g_mix, w_in.astype(BF16), 1024, 1024, 48, "in_proj")
    proj3 = proj2.reshape(bsz, t_len, in_cols)

    w_cat = jnp.concatenate([lru_wa, lru_wx], axis=-1).astype(BF16)
    b_cat = jnp.concatenate([lru_ba, lru_bx], axis=-1)[:, None, :]
    ya = _lru(proj3, conv_w, conv_b, w_cat, b_cat, lru_lambda, lru_width)

    tri = (jnp.arange(SB_TILE)[:, None] > jnp.arange(SB_TILE)[None, :]).astype(BF16)
    tri2 = jnp.concatenate([tri, tri], axis=0)
    yb = _sb_attention(proj3, jnp.tile(sb_gq, 2)[None, :], jnp.tile(sb_gk, 2)[None, :], tri2,
                       q_sb_col, k_sb_col, v_sb_col, sb_width // SB_HEAD_DIM)

    memkv = _norm_matmul(mem.reshape(bsz * m_len, d), g_mem, mem_w_kv.astype(BF16), 1024, 1024, 48, "mem_kv")
    yc = _mem_attention(proj3, memkv.reshape(bsz, m_len, 2 * mem_width), mem_gq, mem_gk, q_mem_col)

    merged = _merge(ya.reshape(n, lru_width), yb.reshape(n, sb_width), yc.reshape(n, mem_width),
                    w_pa.astype(BF16), w_pb.astype(BF16), w_pc.astype(BF16), proj2, b_gate, gate_col, d)
    x1, h2 = _out_proj(x2, merged, w_out.astype(BF16), g_ffn)
    out = _ffn(h2, w_fc.astype(BF16), w_down.astype(BF16), x1)
    return out.reshape(bsz, t_len, d)


def kernel(x, mem, g_mix, g_mem, w_in, b_gate, conv_w, conv_b, lru_wa, lru_ba, lru_wx, lru_bx,
           lru_lambda, sb_gq, sb_gk, mem_w_kv, mem_gq, mem_gk, w_pa, w_pb, w_pc, w_out,
           g_ffn, w_fc, w_down):
    for l in range(g_mix.shape[0]):
        x = _layer(x, mem, g_mix[l], g_mem[l], w_in[l], b_gate[l], conv_w[l], conv_b[l], lru_wa[l],
                   lru_ba[l], lru_wx[l], lru_bx[l], lru_lambda[l], sb_gq[l], sb_gk[l], mem_w_kv[l],
                   mem_gq[l], mem_gk[l], w_pa[l], w_pb[l], w_pc[l], w_out[l], g_ffn[l], w_fc[l],
                   w_down[l])
    return x
```

```python
import jax
import jax.numpy as jnp
from jax import lax
from jax.experimental import pallas as pl
from jax.experimental.pallas import tpu as pltpu

F32 = jnp.float32
BF16 = jnp.bfloat16

NORM_EPS = 1e-6
LRU_C = 8.0
CONV_WIDTH = 4
LRU_BLOCK = 128
SB_HEAD_DIM = 64
MEM_HEAD_DIM = 256
MEM_HEADS = 4

LANES = 128
SUBLANES = 8
MIB = 1024 * 1024

F32_MIN_NORMAL = 1.1754943508222875e-38

EXP_ZERO_BELOW = -104.0


def _params(semantics, vmem_mib):
    return pltpu.CompilerParams(dimension_semantics=semantics, vmem_limit_bytes=vmem_mib * MIB)


NORM_CHUNK = 64


def _rms_rows(x, g):
    ms = jnp.mean(x * x, axis=-1, keepdims=True)
    return x * lax.rsqrt(ms + NORM_EPS) * g


def _norm_matmul_kernel(x_ref, g_ref, w_ref, o_ref, h_ref):
    @pl.when(pl.program_id(1) == 0)
    def _():
        g = g_ref[...]

        def body(c, carry):
            rows = pl.ds(pl.multiple_of(c * NORM_CHUNK, NORM_CHUNK), NORM_CHUNK)
            h_ref[rows, :] = _rms_rows(x_ref[rows, :].astype(F32), g).astype(BF16)
            return carry

        lax.fori_loop(0, x_ref.shape[0] // NORM_CHUNK, body, 0)

    o_ref[...] = jnp.dot(h_ref[...], w_ref[...], preferred_element_type=F32).astype(o_ref.dtype)


def _norm_matmul(x2d, g, w, tm, tn, vmem_mib, name):
    m, k = x2d.shape
    n = w.shape[1]
    return pl.pallas_call(
        _norm_matmul_kernel,
        out_shape=jax.ShapeDtypeStruct((m, n), BF16),
        grid=(m // tm, n // tn),
        in_specs=[
            pl.BlockSpec((tm, k), lambda i, j: (i, 0)),
            pl.BlockSpec((1, k), lambda i, j: (0, 0)),
            pl.BlockSpec((k, tn), lambda i, j: (0, j)),
        ],
        out_specs=pl.BlockSpec((tm, tn), lambda i, j: (i, j)),
        scratch_shapes=[pltpu.VMEM((tm, k), BF16)],
        compiler_params=_params(("arbitrary", "arbitrary"), vmem_mib),
        name=name,
    )(x2d, g.reshape(1, k), w)


LRU_HEADS_PER_STEP = 4
SCAN_CHUNKS = SUBLANES
SCAN_PAD = SUBLANES


def _gelu_tanh(x):
    half = 0.5 * x
    inner = x * (0.7978845608028654 + (0.7978845608028654 * 0.044715) * (x * x))
    return half + half * jnp.tanh(inner)


def _sigmoid(x):
    return 0.5 + 0.5 * jnp.tanh(0.5 * x)


def _sqrt_nonneg(y):
    return jnp.where(y >= F32_MIN_NORMAL, y * lax.rsqrt(jnp.maximum(y, F32_MIN_NORMAL)), 0.0)


def _lru_kernel(u_ref, gate_ref, cw_ref, cb_ref, w_ref, b_ref, lam_ref, o_ref, c_s, a_s, b_s, h_s, p_s):
    t_len = u_ref.shape[1]
    heads = a_s.shape[0]
    chunk = t_len // SCAN_CHUNKS
    pitch = chunk + SCAN_PAD
    row = lax.broadcasted_iota(jnp.int32, (t_len, 1), 0)

    for hh in range(heads):
        lanes = slice(hh * LRU_BLOCK, (hh + 1) * LRU_BLOCK)
        u = u_ref[0, :, lanes].astype(F32)
        c = u * cw_ref[CONV_WIDTH - 1:CONV_WIDTH, lanes]
        for s in range(1, CONV_WIDTH):
            shifted = jnp.where(row >= s, pltpu.roll(u, s, 0), 0.0)
            c = c + shifted * cw_ref[CONV_WIDTH - 1 - s:CONV_WIDTH - s, lanes]
        c_s[:, lanes] = c + cb_ref[:, lanes]

    lam = lam_ref[...]
    neg_sp = -LRU_C * (jnp.maximum(-lam, 0.0) + jnp.log(1.0 + jnp.exp(-jnp.abs(lam))))

    def gates(ci, carry):
        rows = pl.ds(pl.multiple_of(ci * chunk, chunk), chunk)
        dst = pl.ds(pl.multiple_of(ci * pitch, SUBLANES), chunk)
        first = jnp.logical_and(lax.broadcasted_iota(jnp.int32, (chunk, 1), 0) == 0, ci == 0)
        for hh in range(heads):
            lanes = slice(hh * LRU_BLOCK, (hh + 1) * LRU_BLOCK)
            cc = c_s[rows, lanes]
            g = jnp.dot(cc.astype(BF16), w_ref[hh], preferred_element_type=F32) + b_ref[hh]
            r = _sigmoid(g[:, :LRU_BLOCK])
            i = _sigmoid(g[:, LRU_BLOCK:])
            log_a = r * neg_sp[:, lanes]
            a = jnp.exp(log_a)
            mult = _sqrt_nonneg(jnp.maximum(-jnp.tanh(log_a) * (a * a + 1.0), 0.0))
            mult = jnp.where(first, 1.0, mult)
            a_s[hh, dst, :] = a
            b_s[hh, dst, :] = mult * (i * cc)
        return carry

    lax.fori_loop(0, SCAN_CHUNKS, gates, 0)

    def scan(j, state):
        src = pl.ds(j, SCAN_CHUNKS, stride=pitch)
        dst = pl.ds(pl.multiple_of(j * SCAN_CHUNKS, SCAN_CHUNKS), SCAN_CHUNKS)
        new = []
        for hh in range(heads):
            h, p = state[hh]
            a = a_s[hh, src, :]
            h = a * h + b_s[hh, src, :]
            p = a * p
            h_s[hh, dst, :] = h
            p_s[hh, dst, :] = p
            new.append((h, p))
        return tuple(new)

    zeros = jnp.zeros((SCAN_CHUNKS, LRU_BLOCK), F32)
    ends = lax.fori_loop(0, chunk, scan, tuple((zeros, zeros + 1.0) for _ in range(heads)), unroll=8)

    h_in = []
    for hh in range(heads):
        h_end, p_end = ends[hh]
        rows_in = [jnp.zeros((1, LRU_BLOCK), F32)]
        for s in range(1, SCAN_CHUNKS):
            rows_in.append(h_end[s - 1:s, :] + p_end[s - 1:s, :] * rows_in[-1])
        h_in.append(jnp.concatenate(rows_in, axis=0))

    def patch(j, carry):
        src = pl.ds(pl.multiple_of(j * SCAN_CHUNKS, SCAN_CHUNKS), SCAN_CHUNKS)
        dst = pl.ds(j, SCAN_CHUNKS, stride=pitch)
        for hh in range(heads):
            b_s[hh, dst, :] = h_s[hh, src, :] + p_s[hh, src, :] * h_in[hh]
        return carry

    lax.fori_loop(0, chunk, patch, 0, unroll=8)

    def emit(ci, carry):
        rows = pl.ds(pl.multiple_of(ci * chunk, chunk), chunk)
        src = pl.ds(pl.multiple_of(ci * pitch, SUBLANES), chunk)
        for hh in range(heads):
            lanes = slice(hh * LRU_BLOCK, (hh + 1) * LRU_BLOCK)
            gate = gate_ref[0, rows, lanes].astype(F32)
            o_ref[0, rows, lanes] = (b_s[hh, src, :] * _gelu_tanh(gate)).astype(o_ref.dtype)
        return carry

    lax.fori_loop(0, SCAN_CHUNKS, emit, 0)


def _lru(proj3, conv_w, conv_b, w_cat, b_cat, lam, width):
    bsz, t_len, _ = proj3.shape
    hps = LRU_HEADS_PER_STEP
    lw = hps * LRU_BLOCK
    groups = width // lw
    scan_rows = SCAN_CHUNKS * (t_len // SCAN_CHUNKS + SCAN_PAD)
    return pl.pallas_call(
        _lru_kernel,
        out_shape=jax.ShapeDtypeStruct((bsz, t_len, width), BF16),
        grid=(bsz, groups),
        in_specs=[
            pl.BlockSpec((1, t_len, lw), lambda b, h: (b, 0, h)),
            pl.BlockSpec((1, t_len, lw), lambda b, h: (b, 0, groups + h)),
            pl.BlockSpec((CONV_WIDTH, lw), lambda b, h: (0, h)),
            pl.BlockSpec((1, lw), lambda b, h: (0, h)),
            pl.BlockSpec((hps, LRU_BLOCK, 2 * LRU_BLOCK), lambda b, h: (h, 0, 0)),
            pl.BlockSpec((hps, 1, 2 * LRU_BLOCK), lambda b, h: (h, 0, 0)),
            pl.BlockSpec((1, lw), lambda b, h: (0, h)),
        ],
        out_specs=pl.BlockSpec((1, t_len, lw), lambda b, h: (b, 0, h)),
        scratch_shapes=[pltpu.VMEM((t_len, lw), F32),
                        pltpu.VMEM((hps, scan_rows, LRU_BLOCK), F32),
                        pltpu.VMEM((hps, scan_rows, LRU_BLOCK), F32),
                        pltpu.VMEM((hps, t_len, LRU_BLOCK), F32),
                        pltpu.VMEM((hps, t_len, LRU_BLOCK), F32)],
        compiler_params=_params(("arbitrary", "arbitrary"), 48),
        name="lru",
    )(proj3, proj3, conv_w, conv_b.reshape(1, width), w_cat, b_cat, lam.reshape(1, width))


SB_TILE = 256


def _pair_rms(x, g, first_head):
    x2 = x * x
    s0 = jnp.sum(jnp.where(first_head, x2, 0.0), axis=-1, keepdims=True)
    s1 = jnp.sum(jnp.where(first_head, 0.0, x2), axis=-1, keepdims=True)
    rs = jnp.where(first_head, lax.rsqrt(s0 / SB_HEAD_DIM + NORM_EPS), lax.rsqrt(s1 / SB_HEAD_DIM + NORM_EPS))
    return x * rs * g


def _sb_kernel(q_ref, k_ref, v_ref, gq_ref, gk_ref, tri_ref, o_ref, k2_s, v2_s):
    t_len = k_ref.shape[1]
    qi = pl.program_id(2)
    first_head = lax.broadcasted_iota(jnp.int32, (1, LANES), 1) < SB_HEAD_DIM
    zero = jnp.zeros((), BF16)

    @pl.when(qi == 0)
    def _():
        gk = gk_ref[...]

        def body(c, carry):
            rows = pl.ds(pl.multiple_of(c * SB_TILE, SB_TILE), SB_TILE)
            dst0 = pl.ds(pl.multiple_of(c * 2 * SB_TILE, SB_TILE), SB_TILE)
            dst1 = pl.ds(pl.multiple_of(c * 2 * SB_TILE + SB_TILE, SB_TILE), SB_TILE)
            kn = _pair_rms(k_ref[0, rows, :].astype(F32), gk, first_head).astype(BF16)
            vb = v_ref[0, rows, :]
            k2_s[dst0, :] = jnp.where(first_head, kn, zero)
            k2_s[dst1, :] = jnp.where(first_head, zero, kn)
            v2_s[dst0, :] = jnp.where(first_head, vb, zero)
            v2_s[dst1, :] = jnp.where(first_head, zero, vb)
            return carry

        lax.fori_loop(0, t_len // SB_TILE, body, 0)

    scale = SB_HEAD_DIM ** -0.5
    qs = _pair_rms(q_ref[0].astype(F32), gq_ref[...], first_head).astype(BF16) * jnp.asarray(scale, BF16)
    sign_bit = jnp.uint32(0x80000000)
    tri = tri_ref[...]
    r_idx = lax.broadcasted_iota(jnp.int32, (SB_TILE, SB_TILE), 0)
    c_idx = lax.broadcasted_iota(jnp.int32, (SB_TILE, SB_TILE), 1)
    causal = c_idx < r_idx

    def block(kb, masked, acc, carry0, carry1):
        rows = pl.ds(pl.multiple_of(kb * 2 * SB_TILE, 2 * SB_TILE), 2 * SB_TILE)
        z = lax.dot_general(qs, k2_s[rows, :], (((1,), (1,)), ((), ())), preferred_element_type=F32)
        ws = []
        sums = []
        for h in range(2):
            zh = z[:, h * SB_TILE:(h + 1) * SB_TILE]
            neg_abs = lax.bitcast_convert_type(lax.bitcast_convert_type(zh, jnp.uint32) | sign_bit, F32)
            ls = jnp.minimum(zh, 0.0) - jnp.log(1.0 + jnp.exp(neg_abs))
            l1m = ls - zh
            if masked:
                l1m = jnp.where(causal, l1m, 0.0)
            hi = l1m.astype(BF16)
            lo = (l1m - hi.astype(F32)).astype(BF16)
            tail = jnp.dot(jnp.concatenate([hi, lo], axis=1), tri, preferred_element_type=F32)
            w = jnp.exp(ls + tail)
            if masked:
                w = jnp.where(causal, w, 0.0)
            ws.append(w.astype(BF16))
            sums.append(tail[:, 0:1] + l1m[:, 0:1])
        pv = jnp.dot(jnp.concatenate(ws, axis=1), v2_s[rows, :], preferred_element_type=F32)
        later = jnp.where(first_head, jnp.exp(carry0), jnp.exp(carry1))
        return acc + later * pv, carry0 + sums[0], carry1 + sums[1]

    def still_live(carry0, carry1):
        return (jnp.max(jnp.maximum(carry0, carry1)) > EXP_ZERO_BELOW).astype(jnp.int32)

    zcol = jnp.zeros((SB_TILE, 1), F32)
    zacc = jnp.zeros((SB_TILE, LANES), F32)

    def diagonal_only():
        return block(qi, True, zacc, zcol, zcol)

    def diagonal_and_previous():
        return block(qi - 1, False, *block(qi, True, zacc, zcol, zcol))

    acc, carry0, carry1 = lax.cond(qi == 0, diagonal_only, diagonal_and_previous)

    def cond(state):
        kb, live = state[0], state[1]
        return jnp.logical_and(kb >= 0, live > 0)

    def body(state):
        kb, _, acc, carry0, carry1 = state
        acc, carry0, carry1 = block(kb, False, acc, carry0, carry1)
        return kb - 1, still_live(carry0, carry1), acc, carry0, carry1

    state = lax.while_loop(cond, body, (qi - 2, still_live(carry0, carry1), acc, carry0, carry1))
    o_ref[0] = state[2].astype(o_ref.dtype)


def _sb_attention(proj3, gq2, gk2, tri2, q_col, k_col, v_col, heads):
    bsz, t_len, _ = proj3.shape
    pairs = heads // 2
    return pl.pallas_call(
        _sb_kernel,
        out_shape=jax.ShapeDtypeStruct((bsz, t_len, heads * SB_HEAD_DIM), BF16),
        grid=(bsz, pairs, t_len // SB_TILE),
        in_specs=[
            pl.BlockSpec((1, SB_TILE, LANES), lambda b, p, i: (b, i, q_col // LANES + p)),
            pl.BlockSpec((1, t_len, LANES), lambda b, p, i: (b, 0, k_col // LANES + p)),
            pl.BlockSpec((1, t_len, LANES), lambda b, p, i: (b, 0, v_col // LANES + p)),
            pl.BlockSpec((1, LANES), lambda b, p, i: (0, 0)),
            pl.BlockSpec((1, LANES), lambda b, p, i: (0, 0)),
            pl.BlockSpec((2 * SB_TILE, SB_TILE), lambda b, p, i: (0, 0)),
        ],
        out_specs=pl.BlockSpec((1, SB_TILE, LANES), lambda b, p, i: (b, i, p)),
        scratch_shapes=[pltpu.VMEM((2 * t_len, LANES), BF16)] * 2,
        compiler_params=_params(("arbitrary", "arbitrary", "arbitrary"), 32),
        name="sb_attn",
    )(proj3, proj3, proj3, gq2, gk2, tri2)


MEM_TILE = 512


def _mem_attn_kernel(q_ref, kv_ref, gq_ref, gk_ref, o_ref, kn_s):
    width = MEM_HEADS * MEM_HEAD_DIM

    @pl.when(pl.program_id(1) == 0)
    def _():
        for h in range(MEM_HEADS):
            cols = slice(h * MEM_HEAD_DIM, (h + 1) * MEM_HEAD_DIM)
            kn_s[:, cols] = _rms_rows(kv_ref[0, :, cols].astype(F32), gk_ref[...]).astype(BF16)

    scale = MEM_HEAD_DIM ** -0.5
    for h in range(MEM_HEADS):
        cols = slice(h * MEM_HEAD_DIM, (h + 1) * MEM_HEAD_DIM)
        qn = _rms_rows(q_ref[0, :, cols].astype(F32), gq_ref[...]).astype(BF16) * jnp.asarray(scale, BF16)
        s = lax.dot_general(qn, kn_s[:, cols], (((1,), (1,)), ((), ())), preferred_element_type=F32)
        e = jnp.exp(s - jnp.max(s, axis=-1, keepdims=True))
        denom = jnp.sum(e, axis=-1, keepdims=True)
        v = kv_ref[0, :, width + h * MEM_HEAD_DIM:width + (h + 1) * MEM_HEAD_DIM]
        o = jnp.dot(e.astype(BF16), v, preferred_element_type=F32) / denom
        o_ref[0, :, cols] = o.astype(o_ref.dtype)


def _mem_attention(proj3, memkv3, gq, gk, q_col):
    bsz, t_len, _ = proj3.shape
    m_len = memkv3.shape[1]
    width = MEM_HEADS * MEM_HEAD_DIM
    return pl.pallas_call(
        _mem_attn_kernel,
        out_shape=jax.ShapeDtypeStruct((bsz, t_len, width), BF16),
        grid=(bsz, t_len // MEM_TILE),
        in_specs=[
            pl.BlockSpec((1, MEM_TILE, width), lambda b, i: (b, i, q_col // width)),
            pl.BlockSpec((1, m_len, 2 * width), lambda b, i: (b, 0, 0)),
            pl.BlockSpec((1, MEM_HEAD_DIM), lambda b, i: (0, 0)),
            pl.BlockSpec((1, MEM_HEAD_DIM), lambda b, i: (0, 0)),
        ],
        out_specs=pl.BlockSpec((1, MEM_TILE, width), lambda b, i: (b, i, 0)),
        scratch_shapes=[pltpu.VMEM((m_len, width), BF16)],
        compiler_params=_params(("arbitrary", "arbitrary"), 32),
        name="mem_attn",
    )(proj3, memkv3, gq.reshape(1, MEM_HEAD_DIM), gk.reshape(1, MEM_HEAD_DIM))


MERGE_TM = 512
MERGE_TN = 1024


def _merge_kernel(ya_ref, yb_ref, yc_ref, wa_ref, wb_ref, wc_ref, g0_ref, g1_ref, g2_ref,
                  b0_ref, b1_ref, b2_ref, o_ref):
    out = None
    for y_ref, w_ref, g_ref, b_ref in ((ya_ref, wa_ref, g0_ref, b0_ref), (yb_ref, wb_ref, g1_ref, b1_ref),
                                       (yc_ref, wc_ref, g2_ref, b2_ref)):
        gate = jax.nn.sigmoid(g_ref[...].astype(F32) + b_ref[...])
        term = gate * jnp.dot(y_ref[...], w_ref[...], preferred_element_type=F32)
        out = term if out is None else out + term
    o_ref[...] = out.astype(o_ref.dtype)


def _merge(ya, yb, yc, w_pa, w_pb, w_pc, proj2, b_gate, gate_col, d_model):
    n = ya.shape[0]
    tm, tn = MERGE_TM, MERGE_TN
    gate_blk = gate_col // tn
    per_branch = d_model // tn

    def y_spec(width):
        return pl.BlockSpec((tm, width), lambda i, j: (i, 0))

    def w_spec(width):
        return pl.BlockSpec((width, tn), lambda i, j: (0, j))

    def g_spec(branch):
        return pl.BlockSpec((tm, tn), lambda i, j: (i, gate_blk + branch * per_branch + j))

    def b_spec(branch):
        return pl.BlockSpec((1, tn), lambda i, j: (0, branch * per_branch + j))

    b_gate2 = b_gate.reshape(1, -1)
    return pl.pallas_call(
        _merge_kernel,
        out_shape=jax.ShapeDtypeStruct((n, d_model), BF16),
        grid=(n // tm, d_model // tn),
        in_specs=[y_spec(ya.shape[1]), y_spec(yb.shape[1]), y_spec(yc.shape[1]),
                  w_spec(w_pa.shape[0]), w_spec(w_pb.shape[0]), w_spec(w_pc.shape[0]),
                  g_spec(0), g_spec(1), g_spec(2), b_spec(0), b_spec(1), b_spec(2)],
        out_specs=pl.BlockSpec((tm, tn), lambda i, j: (i, j)),
        compiler_params=_params(("arbitrary", "arbitrary"), 48),
        name="merge",
    )(ya, yb, yc, w_pa, w_pb, w_pc, proj2, proj2, proj2, b_gate2, b_gate2, b_gate2)


OUT_TM = 512


def _out_proj_kernel(x_ref, m_ref, w_ref, g_ref, x1_ref, h2_ref):
    x1_ref[...] = x_ref[...] + jnp.dot(m_ref[...], w_ref[...], preferred_element_type=F32)
    g = g_ref[...]

    def body(c, carry):
        rows = pl.ds(pl.multiple_of(c * NORM_CHUNK, NORM_CHUNK), NORM_CHUNK)
        h2_ref[rows, :] = _rms_rows(x1_ref[rows, :], g).astype(BF16)
        return carry

    lax.fori_loop(0, x_ref.shape[0] // NORM_CHUNK, body, 0)


def _out_proj(x2, merged, w_out, g_ffn):
    n, d = x2.shape
    tm = OUT_TM
    row = pl.BlockSpec((tm, d), lambda i: (i, 0))
    return pl.pallas_call(
        _out_proj_kernel,
        out_shape=(jax.ShapeDtypeStruct((n, d), F32), jax.ShapeDtypeStruct((n, d), BF16)),
        grid=(n // tm,),
        in_specs=[row, row, pl.BlockSpec((d, d), lambda i: (0, 0)), pl.BlockSpec((1, d), lambda i: (0, 0))],
        out_specs=(row, row),
        compiler_params=_params(("arbitrary",), 48),
        name="out_proj",
    )(x2, merged, w_out, g_ffn.reshape(1, d))


FFN_TM = 1024
FFN_TF = 512
FFN_TN = 512


def _ffn_up_kernel(h_ref, w_ref, o_ref):
    gu = jnp.dot(h_ref[...], w_ref[...], preferred_element_type=F32)
    gate = gu[:, :FFN_TF]
    up = gu[:, FFN_TF:]
    o_ref[...] = (gate * jax.nn.sigmoid(gate) * up).astype(o_ref.dtype)


def _ffn_down_kernel(a_ref, w_ref, x1_ref, o_ref):
    o_ref[...] = x1_ref[...] + jnp.dot(a_ref[...], w_ref[...], preferred_element_type=F32)


def _ffn(h2, w_gu, w_down, x1):
    n, d = h2.shape
    d_ff = w_down.shape[0]
    tm, tf, tn = FFN_TM, FFN_TF, FFN_TN
    act = pl.pallas_call(
        _ffn_up_kernel,
        out_shape=jax.ShapeDtypeStruct((n, d_ff), BF16),
        grid=(n // tm, d_ff // tf),
        in_specs=[pl.BlockSpec((tm, d), lambda i, f: (i, 0)),
                  pl.BlockSpec((d, 2 * tf), lambda i, f: (0, f))],
        out_specs=pl.BlockSpec((tm, tf), lambda i, f: (i, f)),
        compiler_params=_params(("arbitrary", "arbitrary"), 40),
        name="ffn_up",
    )(h2, w_gu)
    return pl.pallas_call(
        _ffn_down_kernel,
        out_shape=jax.ShapeDtypeStruct((n, d), F32),
        grid=(n // tm, d // tn),
        in_specs=[pl.BlockSpec((tm, d_ff), lambda i, j: (i, 0)),
                  pl.BlockSpec((d_ff, tn), lambda i, j: (0, j)),
                  pl.BlockSpec((tm, tn), lambda i, j: (i, j))],
        out_specs=pl.BlockSpec((tm, tn), lambda i, j: (i, j)),
        compiler_params=_params(("arbitrary", "arbitrary"), 56),
        name="ffn_down",
    )(act, w_down, x1)


def _interleave_gate_up(w_fc, tf):
    d, two_dff = w_fc.shape
    n_f = two_dff // 2 // tf
    return w_fc.reshape(d, 2, n_f, tf).transpose(0, 2, 1, 3).reshape(d, two_dff)


def _layer(x, mem, g_mix, g_mem, w_in, b_gate, conv_w, conv_b, lru_wa, lru_ba, lru_wx, lru_bx,
           lru_lambda, sb_gq, sb_gk, mem_w_kv, mem_gq, mem_gk, w_pa, w_pb, w_pc, w_out,
           g_ffn, w_fc, w_down):
    bsz, t_len, d = x.shape
    m_len = mem.shape[1]
    n = bsz * t_len
    lru_width = w_pa.shape[0]
    sb_width = w_pb.shape[0]
    mem_width = w_pc.shape[0]
    in_cols = w_in.shape[1]
    q_sb_col = 2 * lru_width
    k_sb_col = q_sb_col + sb_width
    v_sb_col = k_sb_col + sb_width
    q_mem_col = v_sb_col + sb_width
    gate_col = q_mem_col + mem_width

    x2 = x.reshape(n, d)
    proj2 = _norm_matmul(x2, g_mix, w_in.astype(BF16), 1024, 1024, 48, "in_proj")
    proj3 = proj2.reshape(bsz, t_len, in_cols)

    w_cat = jnp.concatenate([lru_wa, lru_wx], axis=-1).astype(BF16)
    b_cat = jnp.concatenate([lru_ba, lru_bx], axis=-1)[:, None, :]
    ya = _lru(proj3, conv_w, conv_b, w_cat, b_cat, lru_lambda, lru_width)

    tri = (jnp.arange(SB_TILE)[:, None] > jnp.arange(SB_TILE)[None, :]).astype(BF16)
    tri2 = jnp.concatenate([tri, tri], axis=0)
    yb = _sb_attention(proj3, jnp.tile(sb_gq, 2)[None, :], jnp.tile(sb_gk, 2)[None, :], tri2,
                       q_sb_col, k_sb_col, v_sb_col, sb_width // SB_HEAD_DIM)

    memkv = _norm_matmul(mem.reshape(bsz * m_len, d), g_mem, mem_w_kv.astype(BF16), 1024, 1024, 48, "mem_kv")
    yc = _mem_attention(proj3, memkv.reshape(bsz, m_len, 2 * mem_width), mem_gq, mem_gk, q_mem_col)

    merged = _merge(ya.reshape(n, lru_width), yb.reshape(n, sb_width), yc.reshape(n, mem_width),
                    w_pa.astype(BF16), w_pb.astype(BF16), w_pc.astype(BF16), proj2, b_gate, gate_col, d)
    x1, h2 = _out_proj(x2, merged, w_out.astype(BF16), g_ffn)
    out = _ffn(h2, _interleave_gate_up(w_fc, FFN_TF).astype(BF16), w_down.astype(BF16), x1)
    return out.reshape(bsz, t_len, d)


def kernel(x, mem, g_mix, g_mem, w_in, b_gate, conv_w, conv_b, lru_wa, lru_ba, lru_wx, lru_bx,
           lru_lambda, sb_gq, sb_gk, mem_w_kv, mem_gq, mem_gk, w_pa, w_pb, w_pc, w_out,
           g_ffn, w_fc, w_down):
    for l in range(g_mix.shape[0]):
        x = _layer(x, mem, g_mix[l], g_mem[l], w_in[l], b_gate[l], conv_w[l], conv_b[l], lru_wa[l],
                   lru_ba[l], lru_wx[l], lru_bx[l], lru_lambda[l], sb_gq[l], sb_gk[l], mem_w_kv[l],
                   mem_gq[l], mem_gk[l], w_pa[l], w_pb[l], w_pc[l], w_out[l], g_ffn[l], w_fc[l],
                   w_down[l])
    return x
```

```python
import jax
import jax.numpy as jnp
from jax import lax
from jax.experimental import pallas as pl
from jax.experimental.pallas import tpu as pltpu

F32 = jnp.float32
BF16 = jnp.bfloat16

NORM_EPS = 1e-6
LRU_C = 8.0
CONV_WIDTH = 4
LRU_BLOCK = 128
SB_HEAD_DIM = 64
MEM_HEAD_DIM = 256
MEM_HEADS = 4

LANES = 128
SUBLANES = 8
MIB = 1024 * 1024

F32_MIN_NORMAL = 1.1754943508222875e-38

EXP_ZERO_BELOW = -104.0


def _params(semantics, vmem_mib):
    return pltpu.CompilerParams(dimension_semantics=semantics, vmem_limit_bytes=vmem_mib * MIB)


NORM_CHUNK = 64


def _rms_rows(x, g):
    ms = jnp.mean(x * x, axis=-1, keepdims=True)
    return x * lax.rsqrt(ms + NORM_EPS) * g


def _norm_matmul_kernel(x_ref, g_ref, w_ref, o_ref, h_ref):
    @pl.when(pl.program_id(1) == 0)
    def _():
        g = g_ref[...]

        def body(c, carry):
            rows = pl.ds(pl.multiple_of(c * NORM_CHUNK, NORM_CHUNK), NORM_CHUNK)
            h_ref[rows, :] = _rms_rows(x_ref[rows, :].astype(F32), g).astype(BF16)
            return carry

        lax.fori_loop(0, x_ref.shape[0] // NORM_CHUNK, body, 0)

    o_ref[...] = jnp.dot(h_ref[...], w_ref[...], preferred_element_type=F32).astype(o_ref.dtype)


def _norm_matmul(x2d, g, w, tm, tn, vmem_mib, name):
    m, k = x2d.shape
    n = w.shape[1]
    return pl.pallas_call(
        _norm_matmul_kernel,
        out_shape=jax.ShapeDtypeStruct((m, n), BF16),
        grid=(m // tm, n // tn),
        in_specs=[
            pl.BlockSpec((tm, k), lambda i, j: (i, 0)),
            pl.BlockSpec((1, k), lambda i, j: (0, 0)),
            pl.BlockSpec((k, tn), lambda i, j: (0, j)),
        ],
        out_specs=pl.BlockSpec((tm, tn), lambda i, j: (i, j)),
        scratch_shapes=[pltpu.VMEM((tm, k), BF16)],
        compiler_params=_params(("arbitrary", "arbitrary"), vmem_mib),
        name=name,
    )(x2d, g.reshape(1, k), w)


LRU_HEADS_PER_STEP = 4
SCAN_CHUNKS = SUBLANES
SCAN_PAD = SUBLANES


def _gelu_tanh(x):
    half = 0.5 * x
    inner = x * (0.7978845608028654 + (0.7978845608028654 * 0.044715) * (x * x))
    return half + half * jnp.tanh(inner)


def _sigmoid(x):
    return 0.5 + 0.5 * jnp.tanh(0.5 * x)


def _sqrt_nonneg(y):
    return jnp.where(y >= F32_MIN_NORMAL, y * lax.rsqrt(jnp.maximum(y, F32_MIN_NORMAL)), 0.0)


def _lru_kernel(u_ref, gate_ref, cw_ref, cb_ref, w_ref, b_ref, lam_ref, o_ref, c_s, a_s, b_s, h_s, p_s):
    t_len = u_ref.shape[1]
    heads = a_s.shape[0]
    chunk = t_len // SCAN_CHUNKS
    pitch = chunk + SCAN_PAD
    row = lax.broadcasted_iota(jnp.int32, (t_len, 1), 0)

    for hh in range(heads):
        lanes = slice(hh * LRU_BLOCK, (hh + 1) * LRU_BLOCK)
        u = u_ref[0, :, lanes].astype(F32)
        c = u * cw_ref[CONV_WIDTH - 1:CONV_WIDTH, lanes]
        for s in range(1, CONV_WIDTH):
            shifted = jnp.where(row >= s, pltpu.roll(u, s, 0), 0.0)
            c = c + shifted * cw_ref[CONV_WIDTH - 1 - s:CONV_WIDTH - s, lanes]
        c_s[:, lanes] = c + cb_ref[:, lanes]

    lam = lam_ref[...]
    neg_sp = -LRU_C * (jnp.maximum(-lam, 0.0) + jnp.log(1.0 + jnp.exp(-jnp.abs(lam))))

    def gates(ci, carry):
        rows = pl.ds(pl.multiple_of(ci * chunk, chunk), chunk)
        dst = pl.ds(pl.multiple_of(ci * pitch, SUBLANES), chunk)
        first = jnp.logical_and(lax.broadcasted_iota(jnp.int32, (chunk, 1), 0) == 0, ci == 0)
        for hh in range(heads):
            lanes = slice(hh * LRU_BLOCK, (hh + 1) * LRU_BLOCK)
            cc = c_s[rows, lanes]
            g = jnp.dot(cc.astype(BF16), w_ref[hh], preferred_element_type=F32) + b_ref[hh]
            r = _sigmoid(g[:, :LRU_BLOCK])
            i = _sigmoid(g[:, LRU_BLOCK:])
            log_a = r * neg_sp[:, lanes]
            a = jnp.exp(log_a)
            mult = _sqrt_nonneg(jnp.maximum(-jnp.tanh(log_a) * (a * a + 1.0), 0.0))
            mult = jnp.where(first, 1.0, mult)
            a_s[hh, dst, :] = a
            b_s[hh, dst, :] = mult * (i * cc)
        return carry

    lax.fori_loop(0, SCAN_CHUNKS, gates, 0)

    def scan(j, state):
        src = pl.ds(j, SCAN_CHUNKS, stride=pitch)
        dst = pl.ds(pl.multiple_of(j * SCAN_CHUNKS, SCAN_CHUNKS), SCAN_CHUNKS)
        new = []
        for hh in range(heads):
            h, p = state[hh]
            a = a_s[hh, src, :]
            h = a * h + b_s[hh, src, :]
            p = a * p
            h_s[hh, dst, :] = h
            p_s[hh, dst, :] = p
            new.append((h, p))
        return tuple(new)

    zeros = jnp.zeros((SCAN_CHUNKS, LRU_BLOCK), F32)
    ends = lax.fori_loop(0, chunk, scan, tuple((zeros, zeros + 1.0) for _ in range(heads)), unroll=8)

    h_in = []
    for hh in range(heads):
        h_end, p_end = ends[hh]
        rows_in = [jnp.zeros((1, LRU_BLOCK), F32)]
        for s in range(1, SCAN_CHUNKS):
            rows_in.append(h_end[s - 1:s, :] + p_end[s - 1:s, :] * rows_in[-1])
        h_in.append(jnp.concatenate(rows_in, axis=0))

    def patch(j, carry):
        src = pl.ds(pl.multiple_of(j * SCAN_CHUNKS, SCAN_CHUNKS), SCAN_CHUNKS)
        dst = pl.ds(j, SCAN_CHUNKS, stride=pitch)
        for hh in range(heads):
            b_s[hh, dst, :] = h_s[hh, src, :] + p_s[hh, src, :] * h_in[hh]
        return carry

    lax.fori_loop(0, chunk, patch, 0, unroll=8)

    def emit(ci, carry):
        rows = pl.ds(pl.multiple_of(ci * chunk, chunk), chunk)
        src = pl.ds(pl.multiple_of(ci * pitch, SUBLANES), chunk)
        for hh in range(heads):
            lanes = slice(hh * LRU_BLOCK, (hh + 1) * LRU_BLOCK)
            gate = gate_ref[0, rows, lanes].astype(F32)
            o_ref[0, rows, lanes] = (b_s[hh, src, :] * _gelu_tanh(gate)).astype(o_ref.dtype)
        return carry

    lax.fori_loop(0, SCAN_CHUNKS, emit, 0)


def _lru(proj3, conv_w, conv_b, w_cat, b_cat, lam, width):
    bsz, t_len, _ = proj3.shape
    hps = LRU_HEADS_PER_STEP
    lw = hps * LRU_BLOCK
    groups = width // lw
    scan_rows = SCAN_CHUNKS * (t_len // SCAN_CHUNKS + SCAN_PAD)
    return pl.pallas_call(
        _lru_kernel,
        out_shape=jax.ShapeDtypeStruct((bsz, t_len, width), BF16),
        grid=(bsz, groups),
        in_specs=[
            pl.BlockSpec((1, t_len, lw), lambda b, h: (b, 0, h)),
            pl.BlockSpec((1, t_len, lw), lambda b, h: (b, 0, groups + h)),
            pl.BlockSpec((CONV_WIDTH, lw), lambda b, h: (0, h)),
            pl.BlockSpec((1, lw), lambda b, h: (0, h)),
            pl.BlockSpec((hps, LRU_BLOCK, 2 * LRU_BLOCK), lambda b, h: (h, 0, 0)),
            pl.BlockSpec((hps, 1, 2 * LRU_BLOCK), lambda b, h: (h, 0, 0)),
            pl.BlockSpec((1, lw), lambda b, h: (0, h)),
        ],
        out_specs=pl.BlockSpec((1, t_len, lw), lambda b, h: (b, 0, h)),
        scratch_shapes=[pltpu.VMEM((t_len, lw), F32),
                        pltpu.VMEM((hps, scan_rows, LRU_BLOCK), F32),
                        pltpu.VMEM((hps, scan_rows, LRU_BLOCK), F32),
                        pltpu.VMEM((hps, t_len, LRU_BLOCK), F32),
                        pltpu.VMEM((hps, t_len, LRU_BLOCK), F32)],
        compiler_params=_params(("arbitrary", "arbitrary"), 48),
        name="lru",
    )(proj3, proj3, conv_w, conv_b.reshape(1, width), w_cat, b_cat, lam.reshape(1, width))


SB_TILE = 256
SB_TILES_PER_STEP = 2


def _pair_rms(x, g, first_head):
    x2 = x * x
    s0 = jnp.sum(jnp.where(first_head, x2, 0.0), axis=-1, keepdims=True)
    s1 = jnp.sum(jnp.where(first_head, 0.0, x2), axis=-1, keepdims=True)
    rs = jnp.where(first_head, lax.rsqrt(s0 / SB_HEAD_DIM + NORM_EPS), lax.rsqrt(s1 / SB_HEAD_DIM + NORM_EPS))
    return x * rs * g


def _sb_kernel(q_ref, k_ref, v_ref, gq_ref, gk_ref, tri_ref, o_ref, k2_s, v2_s):
    t_len = k_ref.shape[1]
    qi = pl.program_id(2)
    first_head = lax.broadcasted_iota(jnp.int32, (1, LANES), 1) < SB_HEAD_DIM
    zero = jnp.zeros((), BF16)

    @pl.when(qi == 0)
    def _():
        gk = gk_ref[...]

        def body(c, carry):
            rows = pl.ds(pl.multiple_of(c * SB_TILE, SB_TILE), SB_TILE)
            dst0 = pl.ds(pl.multiple_of(c * 2 * SB_TILE, SB_TILE), SB_TILE)
            dst1 = pl.ds(pl.multiple_of(c * 2 * SB_TILE + SB_TILE, SB_TILE), SB_TILE)
            kn = _pair_rms(k_ref[0, rows, :].astype(F32), gk, first_head).astype(BF16)
            vb = v_ref[0, rows, :]
            k2_s[dst0, :] = jnp.where(first_head, kn, zero)
            k2_s[dst1, :] = jnp.where(first_head, zero, kn)
            v2_s[dst0, :] = jnp.where(first_head, vb, zero)
            v2_s[dst1, :] = jnp.where(first_head, zero, vb)
            return carry

        lax.fori_loop(0, t_len // SB_TILE, body, 0)

    scale = SB_HEAD_DIM ** -0.5
    qs_all = _pair_rms(q_ref[0].astype(F32), gq_ref[...], first_head).astype(BF16) * jnp.asarray(scale, BF16)
    qs_tiles = [qs_all[t * SB_TILE:(t + 1) * SB_TILE] for t in range(SB_TILES_PER_STEP)]
    sign_bit = jnp.uint32(0x80000000)
    tri = tri_ref[...]
    r_idx = lax.broadcasted_iota(jnp.int32, (SB_TILE, SB_TILE), 0)
    c_idx = lax.broadcasted_iota(jnp.int32, (SB_TILE, SB_TILE), 1)
    causal = c_idx < r_idx

    def logits(qs, kb, masked):
        rows = pl.ds(pl.multiple_of(kb * 2 * SB_TILE, 2 * SB_TILE), 2 * SB_TILE)
        z = lax.dot_general(qs, k2_s[rows, :], (((1,), (1,)), ((), ())), preferred_element_type=F32)
        out = []
        for h in range(2):
            zh = z[:, h * SB_TILE:(h + 1) * SB_TILE]
            neg_abs = lax.bitcast_convert_type(lax.bitcast_convert_type(zh, jnp.uint32) | sign_bit, F32)
            ls = jnp.minimum(zh, 0.0) - jnp.log(1.0 + jnp.exp(neg_abs))
            l1m = ls - zh
            if masked:
                l1m = jnp.where(causal, l1m, 0.0)
            out.append((ls, l1m))
        return out

    def weights(ls_l1m, masked):
        ws, sums = [], []
        for ls, l1m in ls_l1m:
            tail = jnp.dot(l1m.astype(BF16), tri, preferred_element_type=F32)
            w = jnp.exp(ls + tail)
            if masked:
                w = jnp.where(causal, w, 0.0)
            ws.append(w.astype(BF16))
            sums.append(tail[:, 0:1] + l1m[:, 0:1])
        return jnp.concatenate(ws, axis=1), sums

    def accumulate(kb, w, sums, acc, carry0, carry1):
        rows = pl.ds(pl.multiple_of(kb * 2 * SB_TILE, 2 * SB_TILE), 2 * SB_TILE)
        pv = jnp.dot(w, v2_s[rows, :], preferred_element_type=F32)
        later = jnp.where(first_head, jnp.exp(carry0), jnp.exp(carry1))
        return acc + later * pv, carry0 + sums[0], carry1 + sums[1]

    def still_live(carry0, carry1):
        return (jnp.max(jnp.maximum(carry0, carry1)) > EXP_ZERO_BELOW).astype(jnp.int32)

    zcol = jnp.zeros((SB_TILE, 1), F32)
    zacc = jnp.zeros((SB_TILE, LANES), F32)
    first_tile = qi * SB_TILES_PER_STEP

    def region(units):
        lg = [[logits(qs_tiles[t], kb, m) for kb, m in blocks] for t, blocks in enumerate(units)]
        wt = [[weights(l, m) for l, (_, m) in zip(lg[t], blocks)] for t, blocks in enumerate(units)]
        out = []
        for t, blocks in enumerate(units):
            state = (zacc, zcol, zcol)
            for (kb, _), (w, sums) in zip(blocks, wt[t]):
                state = accumulate(kb, w, sums, *state)
            out.extend(state)
        return tuple(out)

    def first_step():
        return region([[(first_tile + t - j, j == 0) for j in range(min(t, 1) + 1)]
                       for t in range(SB_TILES_PER_STEP)])

    def later_step():
        return region([[(first_tile + t, True), (first_tile + t - 1, False)]
                       for t in range(SB_TILES_PER_STEP)])

    states = lax.cond(qi == 0, first_step, later_step)

    def cond(state):
        kb, live = state[0], state[1]
        return jnp.logical_and(kb >= 0, live > 0)

    for t in range(SB_TILES_PER_STEP):
        acc, carry0, carry1 = states[3 * t:3 * t + 3]

        def body(state, qs=qs_tiles[t]):
            kb, _, acc, carry0, carry1 = state
            w, sums = weights(logits(qs, kb, False), False)
            acc, carry0, carry1 = accumulate(kb, w, sums, acc, carry0, carry1)
            return kb - 1, still_live(carry0, carry1), acc, carry0, carry1

        state = lax.while_loop(cond, body, (first_tile + t - 2, still_live(carry0, carry1), acc, carry0, carry1))
        o_ref[0, t * SB_TILE:(t + 1) * SB_TILE, :] = state[2].astype(o_ref.dtype)


def _sb_attention(proj3, gq2, gk2, tri2, q_col, k_col, v_col, heads):
    bsz, t_len, _ = proj3.shape
    pairs = heads // 2
    q_rows = SB_TILE * SB_TILES_PER_STEP
    return pl.pallas_call(
        _sb_kernel,
        out_shape=jax.ShapeDtypeStruct((bsz, t_len, heads * SB_HEAD_DIM), BF16),
        grid=(bsz, pairs, t_len // q_rows),
        in_specs=[
            pl.BlockSpec((1, q_rows, LANES), lambda b, p, i: (b, i, q_col // LANES + p)),
            pl.BlockSpec((1, t_len, LANES), lambda b, p, i: (b, 0, k_col // LANES + p)),
            pl.BlockSpec((1, t_len, LANES), lambda b, p, i: (b, 0, v_col // LANES + p)),
            pl.BlockSpec((1, LANES), lambda b, p, i: (0, 0)),
            pl.BlockSpec((1, LANES), lambda b, p, i: (0, 0)),
            pl.BlockSpec((SB_TILE, SB_TILE), lambda b, p, i: (0, 0)),
        ],
        out_specs=pl.BlockSpec((1, q_rows, LANES), lambda b, p, i: (b, i, p)),
        scratch_shapes=[pltpu.VMEM((2 * t_len, LANES), BF16)] * 2,
        compiler_params=_params(("arbitrary", "arbitrary", "arbitrary"), 32),
        name="sb_attn",
    )(proj3, proj3, proj3, gq2, gk2, tri2)


MEM_TILE = 512


def _mem_attn_kernel(q_ref, kv_ref, gq_ref, gk_ref, o_ref, kn_s):
    width = MEM_HEADS * MEM_HEAD_DIM

    @pl.when(pl.program_id(1) == 0)
    def _():
        for h in range(MEM_HEADS):
            cols = slice(h * MEM_HEAD_DIM, (h + 1) * MEM_HEAD_DIM)
            kn_s[:, cols] = _rms_rows(kv_ref[0, :, cols].astype(F32), gk_ref[...]).astype(BF16)

    scale = MEM_HEAD_DIM ** -0.5
    for h in range(MEM_HEADS):
        cols = slice(h * MEM_HEAD_DIM, (h + 1) * MEM_HEAD_DIM)
        qn = _rms_rows(q_ref[0, :, cols].astype(F32), gq_ref[...]).astype(BF16) * jnp.asarray(scale, BF16)
        s = lax.dot_general(qn, kn_s[:, cols], (((1,), (1,)), ((), ())), preferred_element_type=F32)
        e = jnp.exp(s - jnp.max(s, axis=-1, keepdims=True))
        denom = jnp.sum(e, axis=-1, keepdims=True)
        v = kv_ref[0, :, width + h * MEM_HEAD_DIM:width + (h + 1) * MEM_HEAD_DIM]
        o = jnp.dot(e.astype(BF16), v, preferred_element_type=F32) / denom
        o_ref[0, :, cols] = o.astype(o_ref.dtype)


def _mem_attention(proj3, memkv3, gq, gk, q_col):
    bsz, t_len, _ = proj3.shape
    m_len = memkv3.shape[1]
    width = MEM_HEADS * MEM_HEAD_DIM
    return pl.pallas_call(
        _mem_attn_kernel,
        out_shape=jax.ShapeDtypeStruct((bsz, t_len, width), BF16),
        grid=(bsz, t_len // MEM_TILE),
        in_specs=[
            pl.BlockSpec((1, MEM_TILE, width), lambda b, i: (b, i, q_col // width)),
            pl.BlockSpec((1, m_len, 2 * width), lambda b, i: (b, 0, 0)),
            pl.BlockSpec((1, MEM_HEAD_DIM), lambda b, i: (0, 0)),
            pl.BlockSpec((1, MEM_HEAD_DIM), lambda b, i: (0, 0)),
        ],
        out_specs=pl.BlockSpec((1, MEM_TILE, width), lambda b, i: (b, i, 0)),
        scratch_shapes=[pltpu.VMEM((m_len, width), BF16)],
        compiler_params=_params(("arbitrary", "arbitrary"), 32),
        name="mem_attn",
    )(proj3, memkv3, gq.reshape(1, MEM_HEAD_DIM), gk.reshape(1, MEM_HEAD_DIM))


MERGE_TM = 1024
MERGE_TN = 512


def _merge_kernel(ya_ref, yb_ref, yc_ref, wa_ref, wb_ref, wc_ref, g0_ref, g1_ref, g2_ref,
                  b0_ref, b1_ref, b2_ref, o_ref):
    out = None
    for y_ref, w_ref, g_ref, b_ref in ((ya_ref, wa_ref, g0_ref, b0_ref), (yb_ref, wb_ref, g1_ref, b1_ref),
                                       (yc_ref, wc_ref, g2_ref, b2_ref)):
        gate = _sigmoid(g_ref[...].astype(F32) + b_ref[...])
        term = gate * jnp.dot(y_ref[...], w_ref[...], preferred_element_type=F32)
        out = term if out is None else out + term
    o_ref[...] = out.astype(o_ref.dtype)


def _merge(ya, yb, yc, w_pa, w_pb, w_pc, proj2, b_gate, gate_col, d_model):
    n = ya.shape[0]
    tm, tn = MERGE_TM, MERGE_TN
    gate_blk = gate_col // tn
    per_branch = d_model // tn

    def y_spec(width):
        return pl.BlockSpec((tm, width), lambda i, j: (i, 0))

    def w_spec(width):
        return pl.BlockSpec((width, tn), lambda i, j: (0, j))

    def g_spec(branch):
        return pl.BlockSpec((tm, tn), lambda i, j: (i, gate_blk + branch * per_branch + j))

    def b_spec(branch):
        return pl.BlockSpec((1, tn), lambda i, j: (0, branch * per_branch + j))

    b_gate2 = b_gate.reshape(1, -1)
    return pl.pallas_call(
        _merge_kernel,
        out_shape=jax.ShapeDtypeStruct((n, d_model), BF16),
        grid=(n // tm, d_model // tn),
        in_specs=[y_spec(ya.shape[1]), y_spec(yb.shape[1]), y_spec(yc.shape[1]),
                  w_spec(w_pa.shape[0]), w_spec(w_pb.shape[0]), w_spec(w_pc.shape[0]),
                  g_spec(0), g_spec(1), g_spec(2), b_spec(0), b_spec(1), b_spec(2)],
        out_specs=pl.BlockSpec((tm, tn), lambda i, j: (i, j)),
        compiler_params=_params(("arbitrary", "arbitrary"), 48),
        name="merge",
    )(ya, yb, yc, w_pa, w_pb, w_pc, proj2, proj2, proj2, b_gate2, b_gate2, b_gate2)


OUT_TM = 512


def _out_proj_kernel(x_ref, m_ref, w_ref, g_ref, x1_ref, h2_ref):
    x1_ref[...] = x_ref[...] + jnp.dot(m_ref[...], w_ref[...], preferred_element_type=F32)
    g = g_ref[...]

    def body(c, carry):
        rows = pl.ds(pl.multiple_of(c * NORM_CHUNK, NORM_CHUNK), NORM_CHUNK)
        h2_ref[rows, :] = _rms_rows(x1_ref[rows, :], g).astype(BF16)
        return carry

    lax.fori_loop(0, x_ref.shape[0] // NORM_CHUNK, body, 0)


def _out_proj(x2, merged, w_out, g_ffn):
    n, d = x2.shape
    tm = OUT_TM
    row = pl.BlockSpec((tm, d), lambda i: (i, 0))
    return pl.pallas_call(
        _out_proj_kernel,
        out_shape=(jax.ShapeDtypeStruct((n, d), F32), jax.ShapeDtypeStruct((n, d), BF16)),
        grid=(n // tm,),
        in_specs=[row, row, pl.BlockSpec((d, d), lambda i: (0, 0)), pl.BlockSpec((1, d), lambda i: (0, 0))],
        out_specs=(row, row),
        compiler_params=_params(("arbitrary",), 48),
        name="out_proj",
    )(x2, merged, w_out, g_ffn.reshape(1, d))


FFN_TM = 1024
FFN_TF = 512
FFN_TN = 512


def _ffn_up_kernel(h_ref, wg_ref, wu_ref, o_ref):
    h = h_ref[...]
    gate = jnp.dot(h, wg_ref[...], preferred_element_type=F32)
    up = jnp.dot(h, wu_ref[...], preferred_element_type=F32)
    o_ref[...] = (gate * _sigmoid(gate) * up).astype(o_ref.dtype)


def _ffn_down_kernel(a_ref, w_ref, x1_ref, o_ref):
    o_ref[...] = x1_ref[...] + jnp.dot(a_ref[...], w_ref[...], preferred_element_type=F32)


def _ffn(h2, w_fc, w_down, x1):
    n, d = h2.shape
    d_ff = w_down.shape[0]
    tm, tf, tn = FFN_TM, FFN_TF, FFN_TN
    n_f = d_ff // tf
    act = pl.pallas_call(
        _ffn_up_kernel,
        out_shape=jax.ShapeDtypeStruct((n, d_ff), BF16),
        grid=(n // tm, n_f),
        in_specs=[pl.BlockSpec((tm, d), lambda i, f: (i, 0)),
                  pl.BlockSpec((d, tf), lambda i, f: (0, f)),
                  pl.BlockSpec((d, tf), lambda i, f: (0, n_f + f))],
        out_specs=pl.BlockSpec((tm, tf), lambda i, f: (i, f)),
        compiler_params=_params(("arbitrary", "arbitrary"), 40),
        name="ffn_up",
    )(h2, w_fc, w_fc)
    return pl.pallas_call(
        _ffn_down_kernel,
        out_shape=jax.ShapeDtypeStruct((n, d), F32),
        grid=(n // tm, d // tn),
        in_specs=[pl.BlockSpec((tm, d_ff), lambda i, j: (i, 0)),
                  pl.BlockSpec((d_ff, tn), lambda i, j: (0, j)),
                  pl.BlockSpec((tm, tn), lambda i, j: (i, j))],
        out_specs=pl.BlockSpec((tm, tn), lambda i, j: (i, j)),
        compiler_params=_params(("arbitrary", "arbitrary"), 56),
        name="ffn_down",
    )(act, w_down, x1)


def _layer(x, mem, g_mix, g_mem, w_in, b_gate, conv_w, conv_b, lru_wa, lru_ba, lru_wx, lru_bx,
           lru_lambda, sb_gq, sb_gk, mem_w_kv, mem_gq, mem_gk, w_pa, w_pb, w_pc, w_out,
           g_ffn, w_fc, w_down):
    bsz, t_len, d = x.shape
    m_len = mem.shape[1]
    n = bsz * t_len
    lru_width = w_pa.shape[0]
    sb_width = w_pb.shape[0]
    mem_width = w_pc.shape[0]
    in_cols = w_in.shape[1]
    q_sb_col = 2 * lru_width
    k_sb_col = q_sb_col + sb_width
    v_sb_col = k_sb_col + sb_width
    q_mem_col = v_sb_col + sb_width
    gate_col = q_mem_col + mem_width

    x2 = x.reshape(n, d)
    proj2 = _norm_matmul(x2, g_mix, w_in.astype(BF16), 1024, 1024, 48, "in_proj")
    proj3 = proj2.reshape(bsz, t_len, in_cols)

    w_cat = jnp.concatenate([lru_wa, lru_wx], axis=-1).astype(BF16)
    b_cat = jnp.concatenate([lru_ba, lru_bx], axis=-1)[:, None, :]
    ya = _lru(proj3, conv_w, conv_b, w_cat, b_cat, lru_lambda, lru_width)

    tri = (jnp.arange(SB_TILE)[:, None] > jnp.arange(SB_TILE)[None, :]).astype(BF16)
    yb = _sb_attention(proj3, jnp.tile(sb_gq, 2)[None, :], jnp.tile(sb_gk, 2)[None, :], tri,
                       q_sb_col, k_sb_col, v_sb_col, sb_width // SB_HEAD_DIM)

    memkv = _norm_matmul(mem.reshape(bsz * m_len, d), g_mem, mem_w_kv.astype(BF16), 1024, 1024, 48, "mem_kv")
    yc = _mem_attention(proj3, memkv.reshape(bsz, m_len, 2 * mem_width), mem_gq, mem_gk, q_mem_col)

    merged = _merge(ya.reshape(n, lru_width), yb.reshape(n, sb_width), yc.reshape(n, mem_width),
                    w_pa.astype(BF16), w_pb.astype(BF16), w_pc.astype(BF16), proj2, b_gate, gate_col, d)
    x1, h2 = _out_proj(x2, merged, w_out.astype(BF16), g_ffn)
    out = _ffn(h2, w_fc.astype(BF16), w_down.astype(BF16), x1)
    return out.reshape(bsz, t_len, d)


def kernel(x, mem, g_mix, g_mem, w_in, b_gate, conv_w, conv_b, lru_wa, lru_ba, lru_wx, lru_bx,
           lru_lambda, sb_gq, sb_gk, mem_w_kv, mem_gq, mem_gk, w_pa, w_pb, w_pc, w_out,
           g_ffn, w_fc, w_down):
    for l in range(g_mix.shape[0]):
        x = _layer(x, mem, g_mix[l], g_mem[l], w_in[l], b_gate[l], conv_w[l], conv_b[l], lru_wa[l],
                   lru_ba[l], lru_wx[l], lru_bx[l], lru_lambda[l], sb_gq[l], sb_gk[l], mem_w_kv[l],
                   mem_gq[l], mem_gk[l], w_pa[l], w_pb[l], w_pc[l], w_out[l], g_ffn[l], w_fc[l],
                   w_down[l])
    return x
```

```python
import jax
import jax.numpy as jnp
from jax import lax
from jax.experimental import pallas as pl
from jax.experimental.pallas import tpu as pltpu

F32 = jnp.float32
BF16 = jnp.bfloat16

NORM_EPS = 1e-6
LRU_C = 8.0
CONV_WIDTH = 4
LRU_BLOCK = 128
SB_HEAD_DIM = 64
MEM_HEAD_DIM = 256
MEM_HEADS = 4

LANES = 128
SUBLANES = 8
MIB = 1024 * 1024

F32_MIN_NORMAL = 1.1754943508222875e-38

EXP_ZERO_BELOW = -104.0


def _params(semantics, vmem_mib):
    return pltpu.CompilerParams(dimension_semantics=semantics, vmem_limit_bytes=vmem_mib * MIB)


NORM_CHUNK = 64


def _rms_rows(x, g):
    ms = jnp.mean(x * x, axis=-1, keepdims=True)
    return x * lax.rsqrt(ms + NORM_EPS) * g


def _norm_matmul_kernel(x_ref, g_ref, w_ref, o_ref, h_ref):
    @pl.when(pl.program_id(1) == 0)
    def _():
        g = g_ref[...]

        def body(c, carry):
            rows = pl.ds(pl.multiple_of(c * NORM_CHUNK, NORM_CHUNK), NORM_CHUNK)
            h_ref[rows, :] = _rms_rows(x_ref[rows, :].astype(F32), g).astype(BF16)
            return carry

        lax.fori_loop(0, x_ref.shape[0] // NORM_CHUNK, body, 0, unroll=2)

    o_ref[...] = jnp.dot(h_ref[...], w_ref[...], preferred_element_type=F32).astype(o_ref.dtype)


def _norm_matmul(x2d, g, w, tm, tn, vmem_mib, name):
    m, k = x2d.shape
    n = w.shape[1]
    return pl.pallas_call(
        _norm_matmul_kernel,
        out_shape=jax.ShapeDtypeStruct((m, n), BF16),
        grid=(m // tm, n // tn),
        in_specs=[
            pl.BlockSpec((tm, k), lambda i, j: (i, 0)),
            pl.BlockSpec((1, k), lambda i, j: (0, 0)),
            pl.BlockSpec((k, tn), lambda i, j: (0, j)),
        ],
        out_specs=pl.BlockSpec((tm, tn), lambda i, j: (i, j)),
        scratch_shapes=[pltpu.VMEM((tm, k), BF16)],
        compiler_params=_params(("arbitrary", "arbitrary"), vmem_mib),
        name=name,
    )(x2d, g.reshape(1, k), w)


LRU_HEADS_PER_STEP = 4
SCAN_CHUNKS = SUBLANES
SCAN_PAD = SUBLANES


def _gelu_tanh(x):
    half = 0.5 * x
    inner = x * (0.7978845608028654 + (0.7978845608028654 * 0.044715) * (x * x))
    return half + half * jnp.tanh(inner)


def _sigmoid(x):
    return 0.5 + 0.5 * jnp.tanh(0.5 * x)


def _sqrt_clamped(y):
    return jnp.where(y >= F32_MIN_NORMAL, y * lax.rsqrt(y), 0.0)


def _lru_kernel(u_ref, gate_ref, cw_ref, cb_ref, w_ref, b_ref, lam_ref, o_ref, c_s, a_s, b_s, h_s, p_s):
    t_len = u_ref.shape[1]
    heads = a_s.shape[0]
    chunk = t_len // SCAN_CHUNKS
    pitch = chunk + SCAN_PAD
    row = lax.broadcasted_iota(jnp.int32, (t_len, 1), 0)

    for hh in range(heads):
        lanes = slice(hh * LRU_BLOCK, (hh + 1) * LRU_BLOCK)
        u = u_ref[0, :, lanes].astype(F32)
        c = u * cw_ref[CONV_WIDTH - 1:CONV_WIDTH, lanes]
        for s in range(1, CONV_WIDTH):
            shifted = jnp.where(row >= s, pltpu.roll(u, s, 0), 0.0)
            c = c + shifted * cw_ref[CONV_WIDTH - 1 - s:CONV_WIDTH - s, lanes]
        c_s[:, lanes] = c + cb_ref[:, lanes]

    lam = lam_ref[...]
    neg_sp = -LRU_C * (jnp.maximum(-lam, 0.0) + jnp.log(1.0 + jnp.exp(-jnp.abs(lam))))

    def gates(ci, carry):
        rows = pl.ds(pl.multiple_of(ci * chunk, chunk), chunk)
        dst = pl.ds(pl.multiple_of(ci * pitch, SUBLANES), chunk)
        first = jnp.logical_and(lax.broadcasted_iota(jnp.int32, (chunk, 1), 0) == 0, ci == 0)
        for hh in range(heads):
            lanes = slice(hh * LRU_BLOCK, (hh + 1) * LRU_BLOCK)
            cc = c_s[rows, lanes]
            g = jnp.dot(cc.astype(BF16), w_ref[hh], preferred_element_type=F32) + b_ref[hh]
            r = _sigmoid(g[:, :LRU_BLOCK])
            i = _sigmoid(g[:, LRU_BLOCK:])
            log_a = r * neg_sp[:, lanes]
            a = jnp.exp(log_a)
            mult = _sqrt_clamped(-jnp.tanh(log_a) * (a * a + 1.0))
            mult = jnp.where(first, 1.0, mult)
            a_s[hh, dst, :] = a
            b_s[hh, dst, :] = mult * (i * cc)
        return carry

    lax.fori_loop(0, SCAN_CHUNKS, gates, 0)

    def scan(j, state):
        src = pl.ds(j, SCAN_CHUNKS, stride=pitch)
        dst = pl.ds(pl.multiple_of(j * SCAN_CHUNKS, SCAN_CHUNKS), SCAN_CHUNKS)
        new = []
        for hh in range(heads):
            h, p = state[hh]
            a = a_s[hh, src, :]
            h = a * h + b_s[hh, src, :]
            p = a * p
            h_s[hh, dst, :] = h
            p_s[hh, dst, :] = p
            new.append((h, p))
        return tuple(new)

    zeros = jnp.zeros((SCAN_CHUNKS, LRU_BLOCK), F32)
    ends = lax.fori_loop(0, chunk, scan, tuple((zeros, zeros + 1.0) for _ in range(heads)), unroll=8)

    h_in = []
    for hh in range(heads):
        h_end, p_end = ends[hh]
        rows_in = [jnp.zeros((1, LRU_BLOCK), F32)]
        for s in range(1, SCAN_CHUNKS):
            rows_in.append(h_end[s - 1:s, :] + p_end[s - 1:s, :] * rows_in[-1])
        h_in.append(jnp.concatenate(rows_in, axis=0))

    def patch(j, carry):
        src = pl.ds(pl.multiple_of(j * SCAN_CHUNKS, SCAN_CHUNKS), SCAN_CHUNKS)
        dst = pl.ds(j, SCAN_CHUNKS, stride=pitch)
        for hh in range(heads):
            b_s[hh, dst, :] = h_s[hh, src, :] + p_s[hh, src, :] * h_in[hh]
        return carry

    lax.fori_loop(0, chunk, patch, 0, unroll=8)

    def emit(ci, carry):
        rows = pl.ds(pl.multiple_of(ci * chunk, chunk), chunk)
        src = pl.ds(pl.multiple_of(ci * pitch, SUBLANES), chunk)
        for hh in range(heads):
            lanes = slice(hh * LRU_BLOCK, (hh + 1) * LRU_BLOCK)
            gate = gate_ref[0, rows, lanes].astype(F32)
            o_ref[0, rows, lanes] = (b_s[hh, src, :] * _gelu_tanh(gate)).astype(o_ref.dtype)
        return carry

    lax.fori_loop(0, SCAN_CHUNKS, emit, 0)


def _lru(proj3, conv_w, conv_b, w_cat, b_cat, lam, width):
    bsz, t_len, _ = proj3.shape
    hps = LRU_HEADS_PER_STEP
    lw = hps * LRU_BLOCK
    groups = width // lw
    scan_rows = SCAN_CHUNKS * (t_len // SCAN_CHUNKS + SCAN_PAD)
    return pl.pallas_call(
        _lru_kernel,
        out_shape=jax.ShapeDtypeStruct((bsz, t_len, width), BF16),
        grid=(bsz, groups),
        in_specs=[
            pl.BlockSpec((1, t_len, lw), lambda b, h: (b, 0, h)),
            pl.BlockSpec((1, t_len, lw), lambda b, h: (b, 0, groups + h)),
            pl.BlockSpec((CONV_WIDTH, lw), lambda b, h: (0, h)),
            pl.BlockSpec((1, lw), lambda b, h: (0, h)),
            pl.BlockSpec((hps, LRU_BLOCK, 2 * LRU_BLOCK), lambda b, h: (h, 0, 0)),
            pl.BlockSpec((hps, 1, 2 * LRU_BLOCK), lambda b, h: (h, 0, 0)),
            pl.BlockSpec((1, lw), lambda b, h: (0, h)),
        ],
        out_specs=pl.BlockSpec((1, t_len, lw), lambda b, h: (b, 0, h)),
        scratch_shapes=[pltpu.VMEM((t_len, lw), F32),
                        pltpu.VMEM((hps, scan_rows, LRU_BLOCK), F32),
                        pltpu.VMEM((hps, scan_rows, LRU_BLOCK), F32),
                        pltpu.VMEM((hps, t_len, LRU_BLOCK), F32),
                        pltpu.VMEM((hps, t_len, LRU_BLOCK), F32)],
        compiler_params=_params(("arbitrary", "arbitrary"), 48),
        name="lru",
    )(proj3, proj3, conv_w, conv_b.reshape(1, width), w_cat, b_cat, lam.reshape(1, width))


SB_TILE = 256
SB_TILES_PER_STEP = 2


def _pair_rms(x, g, seg):
    x2 = x * x
    hi = x2.astype(BF16)
    lo = (x2 - hi.astype(F32)).astype(BF16)
    ss = jnp.dot(jnp.concatenate([hi, lo], axis=1), seg, preferred_element_type=F32)
    return x * lax.rsqrt(ss * (1.0 / SB_HEAD_DIM) + NORM_EPS) * g


def _sb_kernel(q_ref, k_ref, v_ref, gq_ref, gk_ref, tri_ref, seg_ref, o_ref,
               k2_s, v2_s, qs_s, acc_s, car_s, live_s):
    t_len = k_ref.shape[1]
    qi = pl.program_id(2)
    first_head = lax.broadcasted_iota(jnp.int32, (1, LANES), 1) < SB_HEAD_DIM
    zero = jnp.zeros((), BF16)

    @pl.when(qi == 0)
    def _():
        gk = gk_ref[...]

        def body(c, carry):
            rows = pl.ds(pl.multiple_of(c * SB_TILE, SB_TILE), SB_TILE)
            dst0 = pl.ds(pl.multiple_of(c * 2 * SB_TILE, SB_TILE), SB_TILE)
            dst1 = pl.ds(pl.multiple_of(c * 2 * SB_TILE + SB_TILE, SB_TILE), SB_TILE)
            kn = _pair_rms(k_ref[0, rows, :].astype(F32), gk, seg_ref[...]).astype(BF16)
            vb = v_ref[0, rows, :]
            k2_s[dst0, :] = jnp.where(first_head, kn, zero)
            k2_s[dst1, :] = jnp.where(first_head, zero, kn)
            v2_s[dst0, :] = jnp.where(first_head, vb, zero)
            v2_s[dst1, :] = jnp.where(first_head, zero, vb)
            return carry

        lax.fori_loop(0, t_len // SB_TILE, body, 0, unroll=4)

    scale = SB_HEAD_DIM ** -0.5
    qs_s[...] = _pair_rms(q_ref[0].astype(F32), gq_ref[...], seg_ref[...]).astype(BF16) * jnp.asarray(scale, BF16)
    sign_bit = jnp.uint32(0x80000000)
    first_tile = qi * SB_TILES_PER_STEP

    def tile_rows(t):
        return slice(t * SB_TILE, (t + 1) * SB_TILE)

    def block_rows(kb):
        return pl.ds(pl.multiple_of(kb * 2 * SB_TILE, 2 * SB_TILE), 2 * SB_TILE)

    def causal_mask():
        return (lax.broadcasted_iota(jnp.int32, (SB_TILE, SB_TILE), 1)
                < lax.broadcasted_iota(jnp.int32, (SB_TILE, SB_TILE), 0))

    def logits(t, kb, causal):
        masked = causal is not None
        z = lax.dot_general(qs_s[tile_rows(t), :], k2_s[block_rows(kb), :], (((1,), (1,)), ((), ())),
                            preferred_element_type=F32)
        out = []
        for h in range(2):
            zh = z[:, h * SB_TILE:(h + 1) * SB_TILE]
            neg_abs = lax.bitcast_convert_type(lax.bitcast_convert_type(zh, jnp.uint32) | sign_bit, F32)
            ls = jnp.minimum(zh, 0.0) - jnp.log(1.0 + jnp.exp(neg_abs))
            l1m = ls - zh
            if masked:
                l1m = jnp.where(causal, l1m, 0.0)
            out.append((ls, l1m))
        return out

    def weights(ls_l1m, causal):
        ws, sums = [], []
        for ls, l1m in ls_l1m:
            tail = jnp.dot(l1m.astype(BF16), tri_ref[...], preferred_element_type=F32)
            w = jnp.exp(ls + tail)
            if causal is not None:
                w = jnp.where(causal, w, 0.0)
            ws.append(w.astype(BF16))
            sums.append(tail[:, 0:1] + l1m[:, 0:1])
        return jnp.concatenate(ws, axis=1), sums

    def weighted_values(kb, w):
        return jnp.dot(w, v2_s[block_rows(kb), :], preferred_element_type=F32)

    def later_factor(carry0, carry1):
        return jnp.where(first_head, jnp.exp(carry0), jnp.exp(carry1))

    def still_live(carry0, carry1):
        return (jnp.max(jnp.maximum(carry0, carry1)) > EXP_ZERO_BELOW).astype(jnp.int32)

    def region(units):
        causal = causal_mask()
        lg = [[logits(t, kb, causal if d else None) for kb, d in blocks] for t, blocks in enumerate(units)]
        wt = [[weights(l, causal if d else None) for l, (_, d) in zip(lg[t], blocks)]
              for t, blocks in enumerate(units)]
        for t, blocks in enumerate(units):
            acc = carry0 = carry1 = None
            for (kb, _), (w, sums) in zip(blocks, wt[t]):
                pv = weighted_values(kb, w)
                if acc is None:
                    acc, carry0, carry1 = pv, sums[0], sums[1]
                else:
                    acc = acc + later_factor(carry0, carry1) * pv
                    carry0, carry1 = carry0 + sums[0], carry1 + sums[1]
            acc_s[tile_rows(t), :] = acc
            car_s[2 * t] = carry0
            car_s[2 * t + 1] = carry1
            live_s[t] = still_live(carry0, carry1)

    @pl.when(qi == 0)
    def _():
        region([[(first_tile + t - j, j == 0) for j in range(min(t, 1) + 1)]
                for t in range(SB_TILES_PER_STEP)])

    @pl.when(qi != 0)
    def _():
        region([[(first_tile + t, True), (first_tile + t - 1, False)] for t in range(SB_TILES_PER_STEP)])

    def cond(state):
        kb, live = state
        return jnp.logical_and(kb >= 0, live > 0)

    for t in range(SB_TILES_PER_STEP):
        def body(state, t=t):
            kb, _ = state
            carry0, carry1 = car_s[2 * t], car_s[2 * t + 1]
            w, sums = weights(logits(t, kb, None), None)
            acc_s[tile_rows(t), :] += later_factor(carry0, carry1) * weighted_values(kb, w)
            carry0, carry1 = carry0 + sums[0], carry1 + sums[1]
            car_s[2 * t] = carry0
            car_s[2 * t + 1] = carry1
            return kb - 1, still_live(carry0, carry1)

        lax.while_loop(cond, body, (first_tile + t - 2, live_s[t]))

    o_ref[0] = acc_s[...].astype(o_ref.dtype)


def _sb_attention(proj3, gq2, gk2, tri2, q_col, k_col, v_col, heads):
    bsz, t_len, _ = proj3.shape
    pairs = heads // 2
    lane_head = jnp.arange(LANES) // SB_HEAD_DIM
    same_head = (lane_head[:, None] == lane_head[None, :]).astype(BF16)
    seg2 = jnp.concatenate([same_head, same_head], axis=0)
    q_rows = SB_TILE * SB_TILES_PER_STEP
    return pl.pallas_call(
        _sb_kernel,
        out_shape=jax.ShapeDtypeStruct((bsz, t_len, heads * SB_HEAD_DIM), BF16),
        grid=(bsz, pairs, t_len // q_rows),
        in_specs=[
            pl.BlockSpec((1, q_rows, LANES), lambda b, p, i: (b, i, q_col // LANES + p)),
            pl.BlockSpec((1, t_len, LANES), lambda b, p, i: (b, 0, k_col // LANES + p)),
            pl.BlockSpec((1, t_len, LANES), lambda b, p, i: (b, 0, v_col // LANES + p)),
            pl.BlockSpec((1, LANES), lambda b, p, i: (0, 0)),
            pl.BlockSpec((1, LANES), lambda b, p, i: (0, 0)),
            pl.BlockSpec((SB_TILE, SB_TILE), lambda b, p, i: (0, 0)),
            pl.BlockSpec((2 * LANES, LANES), lambda b, p, i: (0, 0)),
        ],
        out_specs=pl.BlockSpec((1, q_rows, LANES), lambda b, p, i: (b, i, p)),
        scratch_shapes=[pltpu.VMEM((2 * t_len, LANES), BF16),
                        pltpu.VMEM((2 * t_len, LANES), BF16),
                        pltpu.VMEM((q_rows, LANES), BF16),
                        pltpu.VMEM((q_rows, LANES), F32),
                        pltpu.VMEM((2 * SB_TILES_PER_STEP, SB_TILE, 1), F32),
                        pltpu.SMEM((SB_TILES_PER_STEP,), jnp.int32)],
        compiler_params=_params(("arbitrary", "arbitrary", "arbitrary"), 32),
        name="sb_attn",
    )(proj3, proj3, proj3, gq2, gk2, tri2, seg2)


MEM_TILE = 512


def _mem_attn_kernel(q_ref, kv_ref, gq_ref, gk_ref, o_ref, kn_s):
    width = MEM_HEADS * MEM_HEAD_DIM

    @pl.when(pl.program_id(1) == 0)
    def _():
        for h in range(MEM_HEADS):
            cols = slice(h * MEM_HEAD_DIM, (h + 1) * MEM_HEAD_DIM)
            kn_s[:, cols] = _rms_rows(kv_ref[0, :, cols].astype(F32), gk_ref[...]).astype(BF16)

    scale = MEM_HEAD_DIM ** -0.5
    for h in range(MEM_HEADS):
        cols = slice(h * MEM_HEAD_DIM, (h + 1) * MEM_HEAD_DIM)
        qn = _rms_rows(q_ref[0, :, cols].astype(F32), gq_ref[...]).astype(BF16) * jnp.asarray(scale, BF16)
        s = lax.dot_general(qn, kn_s[:, cols], (((1,), (1,)), ((), ())), preferred_element_type=F32)
        e = jnp.exp(s - jnp.max(s, axis=-1, keepdims=True))
        denom = jnp.sum(e, axis=-1, keepdims=True)
        v = kv_ref[0, :, width + h * MEM_HEAD_DIM:width + (h + 1) * MEM_HEAD_DIM]
        o = jnp.dot(e.astype(BF16), v, preferred_element_type=F32) / denom
        o_ref[0, :, cols] = o.astype(o_ref.dtype)


def _mem_attention(proj3, memkv3, gq, gk, q_col):
    bsz, t_len, _ = proj3.shape
    m_len = memkv3.shape[1]
    width = MEM_HEADS * MEM_HEAD_DIM
    return pl.pallas_call(
        _mem_attn_kernel,
        out_shape=jax.ShapeDtypeStruct((bsz, t_len, width), BF16),
        grid=(bsz, t_len // MEM_TILE),
        in_specs=[
            pl.BlockSpec((1, MEM_TILE, width), lambda b, i: (b, i, q_col // width)),
            pl.BlockSpec((1, m_len, 2 * width), lambda b, i: (b, 0, 0)),
            pl.BlockSpec((1, MEM_HEAD_DIM), lambda b, i: (0, 0)),
            pl.BlockSpec((1, MEM_HEAD_DIM), lambda b, i: (0, 0)),
        ],
        out_specs=pl.BlockSpec((1, MEM_TILE, width), lambda b, i: (b, i, 0)),
        scratch_shapes=[pltpu.VMEM((m_len, width), BF16)],
        compiler_params=_params(("arbitrary", "arbitrary"), 32),
        name="mem_attn",
    )(proj3, memkv3, gq.reshape(1, MEM_HEAD_DIM), gk.reshape(1, MEM_HEAD_DIM))


MERGE_TM = 1024
MERGE_TN = 512


def _merge_kernel(ya_ref, yb_ref, yc_ref, wa_ref, wb_ref, wc_ref, g0_ref, g1_ref, g2_ref,
                  b0_ref, b1_ref, b2_ref, o_ref):
    out = None
    for y_ref, w_ref, g_ref, b_ref in ((ya_ref, wa_ref, g0_ref, b0_ref), (yb_ref, wb_ref, g1_ref, b1_ref),
                                       (yc_ref, wc_ref, g2_ref, b2_ref)):
        gate = _sigmoid(g_ref[...].astype(F32) + b_ref[...])
        term = gate * jnp.dot(y_ref[...], w_ref[...], preferred_element_type=F32)
        out = term if out is None else out + term
    o_ref[...] = out.astype(o_ref.dtype)


def _merge(ya, yb, yc, w_pa, w_pb, w_pc, proj2, b_gate, gate_col, d_model):
    n = ya.shape[0]
    tm, tn = MERGE_TM, MERGE_TN
    gate_blk = gate_col // tn
    per_branch = d_model // tn

    def y_spec(width):
        return pl.BlockSpec((tm, width), lambda i, j: (i, 0))

    def w_spec(width):
        return pl.BlockSpec((width, tn), lambda i, j: (0, j))

    def g_spec(branch):
        return pl.BlockSpec((tm, tn), lambda i, j: (i, gate_blk + branch * per_branch + j))

    def b_spec(branch):
        return pl.BlockSpec((1, tn), lambda i, j: (0, branch * per_branch + j))

    b_gate2 = b_gate.reshape(1, -1)
    return pl.pallas_call(
        _merge_kernel,
        out_shape=jax.ShapeDtypeStruct((n, d_model), BF16),
        grid=(n // tm, d_model // tn),
        in_specs=[y_spec(ya.shape[1]), y_spec(yb.shape[1]), y_spec(yc.shape[1]),
                  w_spec(w_pa.shape[0]), w_spec(w_pb.shape[0]), w_spec(w_pc.shape[0]),
                  g_spec(0), g_spec(1), g_spec(2), b_spec(0), b_spec(1), b_spec(2)],
        out_specs=pl.BlockSpec((tm, tn), lambda i, j: (i, j)),
        compiler_params=_params(("arbitrary", "arbitrary"), 48),
        name="merge",
    )(ya, yb, yc, w_pa, w_pb, w_pc, proj2, proj2, proj2, b_gate2, b_gate2, b_gate2)


OUT_TM = 512


def _out_proj_kernel(x_ref, m_ref, w_ref, g_ref, x1_ref, h2_ref):
    x1_ref[...] = x_ref[...] + jnp.dot(m_ref[...], w_ref[...], preferred_element_type=F32)
    g = g_ref[...]

    def body(c, carry):
        rows = pl.ds(pl.multiple_of(c * NORM_CHUNK, NORM_CHUNK), NORM_CHUNK)
        h2_ref[rows, :] = _rms_rows(x1_ref[rows, :], g).astype(BF16)
        return carry

    lax.fori_loop(0, x_ref.shape[0] // NORM_CHUNK, body, 0, unroll=2)


def _out_proj(x2, merged, w_out, g_ffn):
    n, d = x2.shape
    tm = OUT_TM
    row = pl.BlockSpec((tm, d), lambda i: (i, 0))
    return pl.pallas_call(
        _out_proj_kernel,
        out_shape=(jax.ShapeDtypeStruct((n, d), F32), jax.ShapeDtypeStruct((n, d), BF16)),
        grid=(n // tm,),
        in_specs=[row, row, pl.BlockSpec((d, d), lambda i: (0, 0)), pl.BlockSpec((1, d), lambda i: (0, 0))],
        out_specs=(row, row),
        compiler_params=_params(("arbitrary",), 48),
        name="out_proj",
    )(x2, merged, w_out, g_ffn.reshape(1, d))


FFN_TM = 1024
FFN_TF = 512
FFN_TN = 512


def _ffn_up_kernel(h_ref, wg_ref, wu_ref, o_ref):
    h = h_ref[...]
    gate = jnp.dot(h, wg_ref[...], preferred_element_type=F32)
    up = jnp.dot(h, wu_ref[...], preferred_element_type=F32)
    o_ref[...] = (gate * _sigmoid(gate) * up).astype(o_ref.dtype)


def _ffn_down_kernel(a_ref, w_ref, x1_ref, o_ref):
    o_ref[...] = x1_ref[...] + jnp.dot(a_ref[...], w_ref[...], preferred_element_type=F32)


def _ffn(h2, w_fc, w_down, x1):
    n, d = h2.shape
    d_ff = w_down.shape[0]
    tm, tf, tn = FFN_TM, FFN_TF, FFN_TN
    n_f = d_ff // tf
    act = pl.pallas_call(
        _ffn_up_kernel,
        out_shape=jax.ShapeDtypeStruct((n, d_ff), BF16),
        grid=(n // tm, n_f),
        in_specs=[pl.BlockSpec((tm, d), lambda i, f: (i, 0)),
                  pl.BlockSpec((d, tf), lambda i, f: (0, f)),
                  pl.BlockSpec((d, tf), lambda i, f: (0, n_f + f))],
        out_specs=pl.BlockSpec((tm, tf), lambda i, f: (i, f)),
        compiler_params=_params(("arbitrary", "arbitrary"), 40),
        name="ffn_up",
    )(h2, w_fc, w_fc)
    return pl.pallas_call(
        _ffn_down_kernel,
        out_shape=jax.ShapeDtypeStruct((n, d), F32),
        grid=(n // tm, d // tn),
        in_specs=[pl.BlockSpec((tm, d_ff), lambda i, j: (i, 0)),
                  pl.BlockSpec((d_ff, tn), lambda i, j: (0, j)),
                  pl.BlockSpec((tm, tn), lambda i, j: (i, j))],
        out_specs=pl.BlockSpec((tm, tn), lambda i, j: (i, j)),
        compiler_params=_params(("arbitrary", "arbitrary"), 56),
        name="ffn_down",
    )(act, w_down, x1)


def _layer(x, mem, g_mix, g_mem, w_in, b_gate, conv_w, conv_b, lru_wa, lru_ba, lru_wx, lru_bx,
           lru_lambda, sb_gq, sb_gk, mem_w_kv, mem_gq, mem_gk, w_pa, w_pb, w_pc, w_out,
           g_ffn, w_fc, w_down):
    bsz, t_len, d = x.shape
    m_len = mem.shape[1]
    n = bsz * t_len
    lru_width = w_pa.shape[0]
    sb_width = w_pb.shape[0]
    mem_width = w_pc.shape[0]
    in_cols = w_in.shape[1]
    q_sb_col = 2 * lru_width
    k_sb_col = q_sb_col + sb_width
    v_sb_col = k_sb_col + sb_width
    q_mem_col = v_sb_col + sb_width
    gate_col = q_mem_col + mem_width

    x2 = x.reshape(n, d)
    proj2 = _norm_matmul(x2, g_mix, w_in.astype(BF16), 1024, 1024, 48, "in_proj")
    proj3 = proj2.reshape(bsz, t_len, in_cols)

    w_cat = jnp.concatenate([lru_wa, lru_wx], axis=-1).astype(BF16)
    b_cat = jnp.concatenate([lru_ba, lru_bx], axis=-1)[:, None, :]
    ya = _lru(proj3, conv_w, conv_b, w_cat, b_cat, lru_lambda, lru_width)

    tri = (jnp.arange(SB_TILE)[:, None] > jnp.arange(SB_TILE)[None, :]).astype(BF16)
    yb = _sb_attention(proj3, jnp.tile(sb_gq, 2)[None, :], jnp.tile(sb_gk, 2)[None, :], tri,
                       q_sb_col, k_sb_col, v_sb_col, sb_width // SB_HEAD_DIM)

    memkv = _norm_matmul(mem.reshape(bsz * m_len, d), g_mem, mem_w_kv.astype(BF16), 1024, 1024, 48, "mem_kv")
    yc = _mem_attention(proj3, memkv.reshape(bsz, m_len, 2 * mem_width), mem_gq, mem_gk, q_mem_col)

    merged = _merge(ya.reshape(n, lru_width), yb.reshape(n, sb_width), yc.reshape(n, mem_width),
                    w_pa.astype(BF16), w_pb.astype(BF16), w_pc.astype(BF16), proj2, b_gate, gate_col, d)
    x1, h2 = _out_proj(x2, merged, w_out.astype(BF16), g_ffn)
    out = _ffn(h2, w_fc.astype(BF16), w_down.astype(BF16), x1)
    return out.reshape(bsz, t_len, d)


def kernel(x, mem, g_mix, g_mem, w_in, b_gate, conv_w, conv_b, lru_wa, lru_ba, lru_wx, lru_bx,
           lru_lambda, sb_gq, sb_gk, mem_w_kv, mem_gq, mem_gk, w_pa, w_pb, w_pc, w_out,
           g_ffn, w_fc, w_down):
    for l in range(g_mix.shape[0]):
        x = _layer(x, mem, g_mix[l], g_mem[l], w_in[l], b_gate[l], conv_w[l], conv_b[l], lru_wa[l],
                   lru_ba[l], lru_wx[l], lru_bx[l], lru_lambda[l], sb_gq[l], sb_gk[l], mem_w_kv[l],
                   mem_gq[l], mem_gk[l], w_pa[l], w_pb[l], w_pc[l], w_out[l], g_ffn[l], w_fc[l],
                   w_down[l])
    return x
```

```python
import jax
import jax.numpy as jnp
from jax import lax
from jax.experimental import pallas as pl
from jax.experimental.pallas import tpu as pltpu

F32 = jnp.float32
BF16 = jnp.bfloat16

NORM_EPS = 1e-6
LRU_C = 8.0
CONV_WIDTH = 4
LRU_BLOCK = 128
SB_HEAD_DIM = 64
MEM_HEAD_DIM = 256
MEM_HEADS = 4

LANES = 128
SUBLANES = 8
MIB = 1024 * 1024

F32_MIN_NORMAL = 1.1754943508222875e-38

EXP_ZERO_BELOW = -104.0


def _params(semantics, vmem_mib):
    return pltpu.CompilerParams(dimension_semantics=semantics, vmem_limit_bytes=vmem_mib * MIB)


NORM_CHUNK = 64


def _rms_rows(x, g):
    ms = jnp.mean(x * x, axis=-1, keepdims=True)
    return x * lax.rsqrt(ms + NORM_EPS) * g


def _norm_matmul_kernel(x_ref, g_ref, w_ref, o_ref, h_ref):
    @pl.when(pl.program_id(1) == 0)
    def _():
        g = g_ref[...]

        def body(c, carry):
            rows = pl.ds(pl.multiple_of(c * NORM_CHUNK, NORM_CHUNK), NORM_CHUNK)
            h_ref[rows, :] = _rms_rows(x_ref[rows, :].astype(F32), g).astype(BF16)
            return carry

        lax.fori_loop(0, x_ref.shape[0] // NORM_CHUNK, body, 0, unroll=2)

    o_ref[...] = jnp.dot(h_ref[...], w_ref[...], preferred_element_type=F32).astype(o_ref.dtype)


def _norm_matmul(x2d, g, w, tm, tn, vmem_mib, name):
    m, k = x2d.shape
    n = w.shape[1]
    return pl.pallas_call(
        _norm_matmul_kernel,
        out_shape=jax.ShapeDtypeStruct((m, n), BF16),
        grid=(m // tm, n // tn),
        in_specs=[
            pl.BlockSpec((tm, k), lambda i, j: (i, 0)),
            pl.BlockSpec((1, k), lambda i, j: (0, 0)),
            pl.BlockSpec((k, tn), lambda i, j: (0, j)),
        ],
        out_specs=pl.BlockSpec((tm, tn), lambda i, j: (i, j)),
        scratch_shapes=[pltpu.VMEM((tm, k), BF16)],
        compiler_params=_params(("arbitrary", "arbitrary"), vmem_mib),
        name=name,
    )(x2d, g.reshape(1, k), w)


LRU_HEADS_PER_STEP = 4
SCAN_CHUNKS = SUBLANES
SCAN_PAD = SUBLANES


def _gelu_tanh(x):
    half = 0.5 * x
    inner = x * (0.7978845608028654 + (0.7978845608028654 * 0.044715) * (x * x))
    return half + half * jnp.tanh(inner)


def _sigmoid(x):
    return 0.5 + 0.5 * jnp.tanh(0.5 * x)


def _sqrt_clamped(y):
    return jnp.where(y >= F32_MIN_NORMAL, y * lax.rsqrt(y), 0.0)


def _causal_conv(u, taps, bias, zero_pad):
    c = u * taps[CONV_WIDTH - 1:CONV_WIDTH, :]
    if zero_pad:
        row = lax.broadcasted_iota(jnp.int32, (u.shape[0], 1), 0)
    for s in range(1, CONV_WIDTH):
        shifted = pltpu.roll(u, s, 0)
        if zero_pad:
            shifted = jnp.where(row >= s, shifted, 0.0)
        c = c + shifted * taps[CONV_WIDTH - 1 - s:CONV_WIDTH - s, :]
    return c + bias


def _lru_kernel(u_ref, gate_ref, cw_ref, cb_ref, w_ref, b_ref, lam_ref, o_ref, c_s, a_s, b_s, h_s, p_s):
    t_len = u_ref.shape[1]
    heads = a_s.shape[0]
    chunk = t_len // SCAN_CHUNKS
    pitch = chunk + SCAN_PAD

    for hh in range(heads):
        lanes = slice(hh * LRU_BLOCK, (hh + 1) * LRU_BLOCK)
        taps, bias = cw_ref[:, lanes], cb_ref[:, lanes]
        c_s[:, lanes] = _causal_conv(u_ref[0, :, lanes].astype(F32), taps, bias, False)
        c_s[0:SUBLANES, lanes] = _causal_conv(u_ref[0, 0:2 * SUBLANES, lanes].astype(F32), taps, bias,
                                               True)[0:SUBLANES]

    lam = lam_ref[...]
    neg_sp = -LRU_C * (jnp.maximum(-lam, 0.0) + jnp.log(1.0 + jnp.exp(-jnp.abs(lam))))

    def gates(ci, carry):
        rows = pl.ds(pl.multiple_of(ci * chunk, chunk), chunk)
        dst = pl.ds(pl.multiple_of(ci * pitch, SUBLANES), chunk)
        first = jnp.logical_and(lax.broadcasted_iota(jnp.int32, (chunk, 1), 0) == 0, ci == 0)
        for hh in range(heads):
            lanes = slice(hh * LRU_BLOCK, (hh + 1) * LRU_BLOCK)
            cc = c_s[rows, lanes]
            g = jnp.dot(cc.astype(BF16), w_ref[hh], preferred_element_type=F32) + b_ref[hh]
            r = _sigmoid(g[:, :LRU_BLOCK])
            i = _sigmoid(g[:, LRU_BLOCK:])
            log_a = r * neg_sp[:, lanes]
            a = jnp.exp(log_a)
            mult = _sqrt_clamped(-jnp.tanh(log_a) * (a * a + 1.0))
            mult = jnp.where(first, 1.0, mult)
            a_s[hh, dst, :] = a
            b_s[hh, dst, :] = mult * (i * cc)
        return carry

    lax.fori_loop(0, SCAN_CHUNKS, gates, 0)

    def scan(j, state):
        src = pl.ds(j, SCAN_CHUNKS, stride=pitch)
        dst = pl.ds(pl.multiple_of(j * SCAN_CHUNKS, SCAN_CHUNKS), SCAN_CHUNKS)
        new = []
        for hh in range(heads):
            h, p = state[hh]
            a = a_s[hh, src, :]
            h = a * h + b_s[hh, src, :]
            p = a * p
            h_s[hh, dst, :] = h
            p_s[hh, dst, :] = p
            new.append((h, p))
        return tuple(new)

    zeros = jnp.zeros((SCAN_CHUNKS, LRU_BLOCK), F32)
    ends = lax.fori_loop(0, chunk, scan, tuple((zeros, zeros + 1.0) for _ in range(heads)), unroll=8)

    h_in = []
    for hh in range(heads):
        h_end, p_end = ends[hh]
        rows_in = [jnp.zeros((1, LRU_BLOCK), F32)]
        for s in range(1, SCAN_CHUNKS):
            rows_in.append(h_end[s - 1:s, :] + p_end[s - 1:s, :] * rows_in[-1])
        h_in.append(jnp.concatenate(rows_in, axis=0))

    def patch(j, carry):
        src = pl.ds(pl.multiple_of(j * SCAN_CHUNKS, SCAN_CHUNKS), SCAN_CHUNKS)
        dst = pl.ds(j, SCAN_CHUNKS, stride=pitch)
        for hh in range(heads):
            b_s[hh, dst, :] = h_s[hh, src, :] + p_s[hh, src, :] * h_in[hh]
        return carry

    lax.fori_loop(0, chunk, patch, 0, unroll=8)

    def emit(ci, carry):
        rows = pl.ds(pl.multiple_of(ci * chunk, chunk), chunk)
        src = pl.ds(pl.multiple_of(ci * pitch, SUBLANES), chunk)
        for hh in range(heads):
            lanes = slice(hh * LRU_BLOCK, (hh + 1) * LRU_BLOCK)
            gate = gate_ref[0, rows, lanes].astype(F32)
            o_ref[0, rows, lanes] = (b_s[hh, src, :] * _gelu_tanh(gate)).astype(o_ref.dtype)
        return carry

    lax.fori_loop(0, SCAN_CHUNKS, emit, 0)


def _lru(proj3, conv_w, conv_b, w_cat, b_cat, lam, width):
    bsz, t_len, _ = proj3.shape
    hps = LRU_HEADS_PER_STEP
    lw = hps * LRU_BLOCK
    groups = width // lw
    scan_rows = SCAN_CHUNKS * (t_len // SCAN_CHUNKS + SCAN_PAD)
    return pl.pallas_call(
        _lru_kernel,
        out_shape=jax.ShapeDtypeStruct((bsz, t_len, width), BF16),
        grid=(bsz, groups),
        in_specs=[
            pl.BlockSpec((1, t_len, lw), lambda b, h: (b, 0, h)),
            pl.BlockSpec((1, t_len, lw), lambda b, h: (b, 0, groups + h)),
            pl.BlockSpec((CONV_WIDTH, lw), lambda b, h: (0, h)),
            pl.BlockSpec((1, lw), lambda b, h: (0, h)),
            pl.BlockSpec((hps, LRU_BLOCK, 2 * LRU_BLOCK), lambda b, h: (h, 0, 0)),
            pl.BlockSpec((hps, 1, 2 * LRU_BLOCK), lambda b, h: (h, 0, 0)),
            pl.BlockSpec((1, lw), lambda b, h: (0, h)),
        ],
        out_specs=pl.BlockSpec((1, t_len, lw), lambda b, h: (b, 0, h)),
        scratch_shapes=[pltpu.VMEM((t_len, lw), F32),
                        pltpu.VMEM((hps, scan_rows, LRU_BLOCK), F32),
                        pltpu.VMEM((hps, scan_rows, LRU_BLOCK), F32),
                        pltpu.VMEM((hps, t_len, LRU_BLOCK), F32),
                        pltpu.VMEM((hps, t_len, LRU_BLOCK), F32)],
        compiler_params=_params(("arbitrary", "arbitrary"), 48),
        name="lru",
    )(proj3, proj3, conv_w, conv_b.reshape(1, width), w_cat, b_cat, lam.reshape(1, width))


SB_TILE = 256
SB_TILES_PER_STEP = 4
SB_BLOCKS_AHEAD = 2


def _pair_rms(x, g, seg):
    x2 = x * x
    hi = x2.astype(BF16)
    lo = (x2 - hi.astype(F32)).astype(BF16)
    ss = jnp.dot(jnp.concatenate([hi, lo], axis=1), seg, preferred_element_type=F32)
    return x * lax.rsqrt(ss * (1.0 / SB_HEAD_DIM) + NORM_EPS) * g


def _sb_kernel(q_ref, k_ref, v_ref, gq_ref, gk_ref, tri_ref, seg_ref, o_ref,
               k2_s, v2_s, qs_s, acc_s, car_s, live_s):
    t_len = k_ref.shape[1]
    qi = pl.program_id(2)
    first_head = lax.broadcasted_iota(jnp.int32, (1, LANES), 1) < SB_HEAD_DIM
    zero = jnp.zeros((), BF16)

    @pl.when(qi == 0)
    def _():
        gk = gk_ref[...]

        def body(c, carry):
            rows = pl.ds(pl.multiple_of(c * SB_TILE, SB_TILE), SB_TILE)
            dst0 = pl.ds(pl.multiple_of(c * 2 * SB_TILE, SB_TILE), SB_TILE)
            dst1 = pl.ds(pl.multiple_of(c * 2 * SB_TILE + SB_TILE, SB_TILE), SB_TILE)
            kn = _pair_rms(k_ref[0, rows, :].astype(F32), gk, seg_ref[...]).astype(BF16)
            vb = v_ref[0, rows, :]
            k2_s[dst0, :] = jnp.where(first_head, kn, zero)
            k2_s[dst1, :] = jnp.where(first_head, zero, kn)
            v2_s[dst0, :] = jnp.where(first_head, vb, zero)
            v2_s[dst1, :] = jnp.where(first_head, zero, vb)
            return carry

        lax.fori_loop(0, t_len // SB_TILE, body, 0, unroll=4)

    scale = SB_HEAD_DIM ** -0.5
    qs_s[...] = _pair_rms(q_ref[0].astype(F32), gq_ref[...], seg_ref[...]).astype(BF16) * jnp.asarray(scale, BF16)
    sign_bit = jnp.uint32(0x80000000)
    first_tile = qi * SB_TILES_PER_STEP

    def tile_rows(t):
        return slice(t * SB_TILE, (t + 1) * SB_TILE)

    def block_rows(kb):
        return pl.ds(pl.multiple_of(kb * 2 * SB_TILE, 2 * SB_TILE), 2 * SB_TILE)

    def causal_mask():
        return (lax.broadcasted_iota(jnp.int32, (SB_TILE, SB_TILE), 1)
                < lax.broadcasted_iota(jnp.int32, (SB_TILE, SB_TILE), 0))

    def logits(t, kb, causal):
        z = lax.dot_general(qs_s[tile_rows(t), :], k2_s[block_rows(kb), :], (((1,), (1,)), ((), ())),
                            preferred_element_type=F32)
        out = []
        for h in range(2):
            zh = z[:, h * SB_TILE:(h + 1) * SB_TILE]
            neg_abs = lax.bitcast_convert_type(lax.bitcast_convert_type(zh, jnp.uint32) | sign_bit, F32)
            ls = jnp.minimum(zh, 0.0) - jnp.log(1.0 + jnp.exp(neg_abs))
            l1m = ls - zh
            if causal is not None:
                l1m = jnp.where(causal, l1m, 0.0)
            out.append((ls, l1m))
        return out

    def weights(ls_l1m, causal):
        ws, sums = [], []
        for ls, l1m in ls_l1m:
            tail = jnp.dot(l1m.astype(BF16), tri_ref[...], preferred_element_type=F32)
            w = jnp.exp(ls + tail)
            if causal is not None:
                w = jnp.where(causal, w, 0.0)
            ws.append(w.astype(BF16))
            sums.append(tail[:, 0:1] + l1m[:, 0:1])
        return jnp.concatenate(ws, axis=1), sums

    def weighted_values(kb, w):
        return jnp.dot(w, v2_s[block_rows(kb), :], preferred_element_type=F32)

    def later_factor(carry0, carry1):
        return jnp.where(first_head, jnp.exp(carry0), jnp.exp(carry1))

    def still_live(carry0, carry1):
        return (jnp.max(jnp.maximum(carry0, carry1)) > EXP_ZERO_BELOW).astype(jnp.int32)

    def region(units):
        causal = causal_mask()
        lg = [[logits(t, kb, causal if d else None) for kb, d in blocks] for t, blocks in enumerate(units)]
        wt = [[weights(l, causal if d else None) for l, (_, d) in zip(lg[t], blocks)]
              for t, blocks in enumerate(units)]
        for t, blocks in enumerate(units):
            acc = carry0 = carry1 = None
            for (kb, _), (w, sums) in zip(blocks, wt[t]):
                pv = weighted_values(kb, w)
                if acc is None:
                    acc, carry0, carry1 = pv, sums[0], sums[1]
                else:
                    acc = acc + later_factor(carry0, carry1) * pv
                    carry0, carry1 = carry0 + sums[0], carry1 + sums[1]
            acc_s[tile_rows(t), :] = acc
            car_s[2 * t] = carry0
            car_s[2 * t + 1] = carry1
            live_s[t] = still_live(carry0, carry1)

    @pl.when(qi == 0)
    def _():
        region([[(first_tile + t - j, j == 0) for j in range(min(t + 1, SB_BLOCKS_AHEAD))]
                for t in range(SB_TILES_PER_STEP)])

    @pl.when(qi != 0)
    def _():
        region([[(first_tile + t - j, j == 0) for j in range(SB_BLOCKS_AHEAD)]
                for t in range(SB_TILES_PER_STEP)])

    def cond(state):
        kb, live = state
        return jnp.logical_and(kb >= 0, live > 0)

    for t in range(SB_TILES_PER_STEP):
        def body(state, t=t):
            kb, _ = state
            carry0, carry1 = car_s[2 * t], car_s[2 * t + 1]
            w, sums = weights(logits(t, kb, None), None)
            acc_s[tile_rows(t), :] += later_factor(carry0, carry1) * weighted_values(kb, w)
            carry0, carry1 = carry0 + sums[0], carry1 + sums[1]
            car_s[2 * t] = carry0
            car_s[2 * t + 1] = carry1
            return kb - 1, still_live(carry0, carry1)

        lax.while_loop(cond, body, (first_tile + t - SB_BLOCKS_AHEAD, live_s[t]))

    o_ref[0] = acc_s[...].astype(o_ref.dtype)


def _sb_attention(proj3, gq2, gk2, q_col, k_col, v_col, heads):
    bsz, t_len, _ = proj3.shape
    pairs = heads // 2
    lane_head = jnp.arange(LANES) // SB_HEAD_DIM
    same_head = (lane_head[:, None] == lane_head[None, :]).astype(BF16)
    seg2 = jnp.concatenate([same_head, same_head], axis=0)
    key = jnp.arange(SB_TILE)
    later_key = (key[:, None] > key[None, :]).astype(BF16)
    q_rows = SB_TILE * SB_TILES_PER_STEP
    return pl.pallas_call(
        _sb_kernel,
        out_shape=jax.ShapeDtypeStruct((bsz, t_len, heads * SB_HEAD_DIM), BF16),
        grid=(bsz, pairs, t_len // q_rows),
        in_specs=[
            pl.BlockSpec((1, q_rows, LANES), lambda b, p, i: (b, i, q_col // LANES + p)),
            pl.BlockSpec((1, t_len, LANES), lambda b, p, i: (b, 0, k_col // LANES + p)),
            pl.BlockSpec((1, t_len, LANES), lambda b, p, i: (b, 0, v_col // LANES + p)),
            pl.BlockSpec((1, LANES), lambda b, p, i: (0, 0)),
            pl.BlockSpec((1, LANES), lambda b, p, i: (0, 0)),
            pl.BlockSpec((SB_TILE, SB_TILE), lambda b, p, i: (0, 0)),
            pl.BlockSpec((2 * LANES, LANES), lambda b, p, i: (0, 0)),
        ],
        out_specs=pl.BlockSpec((1, q_rows, LANES), lambda b, p, i: (b, i, p)),
        scratch_shapes=[pltpu.VMEM((2 * t_len, LANES), BF16),
                        pltpu.VMEM((2 * t_len, LANES), BF16),
                        pltpu.VMEM((q_rows, LANES), BF16),
                        pltpu.VMEM((q_rows, LANES), F32),
                        pltpu.VMEM((2 * SB_TILES_PER_STEP, SB_TILE, 1), F32),
                        pltpu.SMEM((SB_TILES_PER_STEP,), jnp.int32)],
        compiler_params=_params(("arbitrary", "arbitrary", "arbitrary"), 32),
        name="sb_attn",
    )(proj3, proj3, proj3, gq2, gk2, later_key, seg2)


MEM_TILE = 512


def _mem_attn_kernel(q_ref, kv_ref, gq_ref, gk_ref, o_ref, kn_s):
    width = MEM_HEADS * MEM_HEAD_DIM

    @pl.when(pl.program_id(1) == 0)
    def _():
        for h in range(MEM_HEADS):
            cols = slice(h * MEM_HEAD_DIM, (h + 1) * MEM_HEAD_DIM)
            kn_s[:, cols] = _rms_rows(kv_ref[0, :, cols].astype(F32), gk_ref[...]).astype(BF16)

    scale = MEM_HEAD_DIM ** -0.5
    for h in range(MEM_HEADS):
        cols = slice(h * MEM_HEAD_DIM, (h + 1) * MEM_HEAD_DIM)
        qn = _rms_rows(q_ref[0, :, cols].astype(F32), gq_ref[...]).astype(BF16) * jnp.asarray(scale, BF16)
        s = lax.dot_general(qn, kn_s[:, cols], (((1,), (1,)), ((), ())), preferred_element_type=F32)
        e = jnp.exp(s - jnp.max(s, axis=-1, keepdims=True))
        denom = jnp.sum(e, axis=-1, keepdims=True)
        v = kv_ref[0, :, width + h * MEM_HEAD_DIM:width + (h + 1) * MEM_HEAD_DIM]
        o = jnp.dot(e.astype(BF16), v, preferred_element_type=F32) / denom
        o_ref[0, :, cols] = o.astype(o_ref.dtype)


def _mem_attention(proj3, memkv3, gq, gk, q_col):
    bsz, t_len, _ = proj3.shape
    m_len = memkv3.shape[1]
    width = MEM_HEADS * MEM_HEAD_DIM
    return pl.pallas_call(
        _mem_attn_kernel,
        out_shape=jax.ShapeDtypeStruct((bsz, t_len, width), BF16),
        grid=(bsz, t_len // MEM_TILE),
        in_specs=[
            pl.BlockSpec((1, MEM_TILE, width), lambda b, i: (b, i, q_col // width)),
            pl.BlockSpec((1, m_len, 2 * width), lambda b, i: (b, 0, 0)),
            pl.BlockSpec((1, MEM_HEAD_DIM), lambda b, i: (0, 0)),
            pl.BlockSpec((1, MEM_HEAD_DIM), lambda b, i: (0, 0)),
        ],
        out_specs=pl.BlockSpec((1, MEM_TILE, width), lambda b, i: (b, i, 0)),
        scratch_shapes=[pltpu.VMEM((m_len, width), BF16)],
        compiler_params=_params(("arbitrary", "arbitrary"), 32),
        name="mem_attn",
    )(proj3, memkv3, gq.reshape(1, MEM_HEAD_DIM), gk.reshape(1, MEM_HEAD_DIM))


MERGE_TM = 1024
MERGE_TN = 512


def _merge_kernel(ya_ref, yb_ref, yc_ref, wa_ref, wb_ref, wc_ref, g0_ref, g1_ref, g2_ref,
                  b0_ref, b1_ref, b2_ref, o_ref):
    out = None
    for y_ref, w_ref, g_ref, b_ref in ((ya_ref, wa_ref, g0_ref, b0_ref), (yb_ref, wb_ref, g1_ref, b1_ref),
                                       (yc_ref, wc_ref, g2_ref, b2_ref)):
        gate = _sigmoid(g_ref[...].astype(F32) + b_ref[...])
        term = gate * jnp.dot(y_ref[...], w_ref[...], preferred_element_type=F32)
        out = term if out is None else out + term
    o_ref[...] = out.astype(o_ref.dtype)


def _merge(ya, yb, yc, w_pa, w_pb, w_pc, proj2, b_gate, gate_col, d_model):
    n = ya.shape[0]
    tm, tn = MERGE_TM, MERGE_TN
    gate_blk = gate_col // tn
    per_branch = d_model // tn

    def y_spec(width):
        return pl.BlockSpec((tm, width), lambda i, j: (i, 0))

    def w_spec(width):
        return pl.BlockSpec((width, tn), lambda i, j: (0, j))

    def g_spec(branch):
        return pl.BlockSpec((tm, tn), lambda i, j: (i, gate_blk + branch * per_branch + j))

    def b_spec(branch):
        return pl.BlockSpec((1, tn), lambda i, j: (0, branch * per_branch + j))

    b_gate2 = b_gate.reshape(1, -1)
    return pl.pallas_call(
        _merge_kernel,
        out_shape=jax.ShapeDtypeStruct((n, d_model), BF16),
        grid=(n // tm, d_model // tn),
        in_specs=[y_spec(ya.shape[1]), y_spec(yb.shape[1]), y_spec(yc.shape[1]),
                  w_spec(w_pa.shape[0]), w_spec(w_pb.shape[0]), w_spec(w_pc.shape[0]),
                  g_spec(0), g_spec(1), g_spec(2), b_spec(0), b_spec(1), b_spec(2)],
        out_specs=pl.BlockSpec((tm, tn), lambda i, j: (i, j)),
        compiler_params=_params(("arbitrary", "arbitrary"), 48),
        name="merge",
    )(ya, yb, yc, w_pa, w_pb, w_pc, proj2, proj2, proj2, b_gate2, b_gate2, b_gate2)


OUT_TM = 512


def _out_proj_kernel(x_ref, m_ref, w_ref, g_ref, x1_ref, h2_ref):
    x1_ref[...] = x_ref[...] + jnp.dot(m_ref[...], w_ref[...], preferred_element_type=F32)
    g = g_ref[...]

    def body(c, carry):
        rows = pl.ds(pl.multiple_of(c * NORM_CHUNK, NORM_CHUNK), NORM_CHUNK)
        h2_ref[rows, :] = _rms_rows(x1_ref[rows, :], g).astype(BF16)
        return carry

    lax.fori_loop(0, x_ref.shape[0] // NORM_CHUNK, body, 0, unroll=2)


def _out_proj(x2, merged, w_out, g_ffn):
    n, d = x2.shape
    tm = OUT_TM
    row = pl.BlockSpec((tm, d), lambda i: (i, 0))
    return pl.pallas_call(
        _out_proj_kernel,
        out_shape=(jax.ShapeDtypeStruct((n, d), F32), jax.ShapeDtypeStruct((n, d), BF16)),
        grid=(n // tm,),
        in_specs=[row, row, pl.BlockSpec((d, d), lambda i: (0, 0)), pl.BlockSpec((1, d), lambda i: (0, 0))],
        out_specs=(row, row),
        compiler_params=_params(("arbitrary",), 48),
        name="out_proj",
    )(x2, merged, w_out, g_ffn.reshape(1, d))


FFN_TM = 1024
FFN_TF = 512
FFN_TN = 512


def _ffn_up_kernel(h_ref, wg_ref, wu_ref, o_ref):
    h = h_ref[...]
    gate = jnp.dot(h, wg_ref[...], preferred_element_type=F32)
    up = jnp.dot(h, wu_ref[...], preferred_element_type=F32)
    o_ref[...] = (gate * _sigmoid(gate) * up).astype(o_ref.dtype)


def _ffn_down_kernel(a_ref, w_ref, x1_ref, o_ref):
    o_ref[...] = x1_ref[...] + jnp.dot(a_ref[...], w_ref[...], preferred_element_type=F32)


def _ffn(h2, w_fc, w_down, x1):
    n, d = h2.shape
    d_ff = w_down.shape[0]
    tm, tf, tn = FFN_TM, FFN_TF, FFN_TN
    n_f = d_ff // tf
    act = pl.pallas_call(
        _ffn_up_kernel,
        out_shape=jax.ShapeDtypeStruct((n, d_ff), BF16),
        grid=(n // tm, n_f),
        in_specs=[pl.BlockSpec((tm, d), lambda i, f: (i, 0)),
                  pl.BlockSpec((d, tf), lambda i, f: (0, f)),
                  pl.BlockSpec((d, tf), lambda i, f: (0, n_f + f))],
        out_specs=pl.BlockSpec((tm, tf), lambda i, f: (i, f)),
        compiler_params=_params(("arbitrary", "arbitrary"), 40),
        name="ffn_up",
    )(h2, w_fc, w_fc)
    return pl.pallas_call(
        _ffn_down_kernel,
        out_shape=jax.ShapeDtypeStruct((n, d), F32),
        grid=(n // tm, d // tn),
        in_specs=[pl.BlockSpec((tm, d_ff), lambda i, j: (i, 0)),
                  pl.BlockSpec((d_ff, tn), lambda i, j: (0, j)),
                  pl.BlockSpec((tm, tn), lambda i, j: (i, j))],
        out_specs=pl.BlockSpec((tm, tn), lambda i, j: (i, j)),
        compiler_params=_params(("arbitrary", "arbitrary"), 56),
        name="ffn_down",
    )(act, w_down, x1)


def _layer(x, mem, g_mix, g_mem, w_in, b_gate, conv_w, conv_b, lru_wa, lru_ba, lru_wx, lru_bx,
           lru_lambda, sb_gq, sb_gk, mem_w_kv, mem_gq, mem_gk, w_pa, w_pb, w_pc, w_out,
           g_ffn, w_fc, w_down):
    bsz, t_len, d = x.shape
    m_len = mem.shape[1]
    n = bsz * t_len
    lru_width = w_pa.shape[0]
    sb_width = w_pb.shape[0]
    mem_width = w_pc.shape[0]
    in_cols = w_in.shape[1]
    q_sb_col = 2 * lru_width
    k_sb_col = q_sb_col + sb_width
    v_sb_col = k_sb_col + sb_width
    q_mem_col = v_sb_col + sb_width
    gate_col = q_mem_col + mem_width

    x2 = x.reshape(n, d)
    proj2 = _norm_matmul(x2, g_mix, w_in.astype(BF16), 1024, 2048, 56, "in_proj")
    proj3 = proj2.reshape(bsz, t_len, in_cols)

    w_cat = jnp.concatenate([lru_wa, lru_wx], axis=-1).astype(BF16)
    b_cat = jnp.concatenate([lru_ba, lru_bx], axis=-1)[:, None, :]
    ya = _lru(proj3, conv_w, conv_b, w_cat, b_cat, lru_lambda, lru_width)

    yb = _sb_attention(proj3, jnp.tile(sb_gq, 2)[None, :], jnp.tile(sb_gk, 2)[None, :],
                       q_sb_col, k_sb_col, v_sb_col, sb_width // SB_HEAD_DIM)

    memkv = _norm_matmul(mem.reshape(bsz * m_len, d), g_mem, mem_w_kv.astype(BF16), 1024, 1024, 48, "mem_kv")
    yc = _mem_attention(proj3, memkv.reshape(bsz, m_len, 2 * mem_width), mem_gq, mem_gk, q_mem_col)

    merged = _merge(ya.reshape(n, lru_width), yb.reshape(n, sb_width), yc.reshape(n, mem_width),
                    w_pa.astype(BF16), w_pb.astype(BF16), w_pc.astype(BF16), proj2, b_gate, gate_col, d)
    x1, h2 = _out_proj(x2, merged, w_out.astype(BF16), g_ffn)
    out = _ffn(h2, w_fc.astype(BF16), w_down.astype(BF16), x1)
    return out.reshape(bsz, t_len, d)


def kernel(x, mem, g_mix, g_mem, w_in, b_gate, conv_w, conv_b, lru_wa, lru_ba, lru_wx, lru_bx,
           lru_lambda, sb_gq, sb_gk, mem_w_kv, mem_gq, mem_gk, w_pa, w_pb, w_pc, w_out,
           g_ffn, w_fc, w_down):
    for l in range(g_mix.shape[0]):
        x = _layer(x, mem, g_mix[l], g_mem[l], w_in[l], b_gate[l], conv_w[l], conv_b[l], lru_wa[l],
                   lru_ba[l], lru_wx[l], lru_bx[l], lru_lambda[l], sb_gq[l], sb_gk[l], mem_w_kv[l],
                   mem_gq[l], mem_gk[l], w_pa[l], w_pb[l], w_pc[l], w_out[l], g_ffn[l], w_fc[l],
                   w_down[l])
    return x
```

```python
import jax
import jax.numpy as jnp
from jax import lax
from jax.experimental import pallas as pl
from jax.experimental.pallas import tpu as pltpu

F32 = jnp.float32
BF16 = jnp.bfloat16

NORM_EPS = 1e-6
LRU_C = 8.0
CONV_WIDTH = 4
LRU_BLOCK = 128
SB_HEAD_DIM = 64
MEM_HEAD_DIM = 256
MEM_HEADS = 4

LANES = 128
SUBLANES = 8
MIB = 1024 * 1024

F32_MIN_NORMAL = 1.1754943508222875e-38

EXP_ZERO_BELOW = -104.0


def _params(semantics, vmem_mib):
    return pltpu.CompilerParams(dimension_semantics=semantics, vmem_limit_bytes=vmem_mib * MIB)


NORM_CHUNK = 64


def _rms_rows(x, g):
    ms = jnp.mean(x * x, axis=-1, keepdims=True)
    return x * lax.rsqrt(ms + NORM_EPS) * g


def _norm_matmul_kernel(x_ref, g_ref, w_ref, o_ref, h_ref):
    @pl.when(pl.program_id(1) == 0)
    def _():
        g = g_ref[...]

        def body(c, carry):
            rows = pl.ds(pl.multiple_of(c * NORM_CHUNK, NORM_CHUNK), NORM_CHUNK)
            h_ref[rows, :] = _rms_rows(x_ref[rows, :].astype(F32), g).astype(BF16)
            return carry

        lax.fori_loop(0, x_ref.shape[0] // NORM_CHUNK, body, 0, unroll=2)

    o_ref[...] = jnp.dot(h_ref[...], w_ref[...], preferred_element_type=F32).astype(o_ref.dtype)


def _norm_matmul(x2d, g, w, tm, tn, vmem_mib, name):
    m, k = x2d.shape
    n = w.shape[1]
    return pl.pallas_call(
        _norm_matmul_kernel,
        out_shape=jax.ShapeDtypeStruct((m, n), BF16),
        grid=(m // tm, n // tn),
        in_specs=[
            pl.BlockSpec((tm, k), lambda i, j: (i, 0)),
            pl.BlockSpec((1, k), lambda i, j: (0, 0)),
            pl.BlockSpec((k, tn), lambda i, j: (0, j)),
        ],
        out_specs=pl.BlockSpec((tm, tn), lambda i, j: (i, j)),
        scratch_shapes=[pltpu.VMEM((tm, k), BF16)],
        compiler_params=_params(("arbitrary", "arbitrary"), vmem_mib),
        name=name,
    )(x2d, g.reshape(1, k), w)


LRU_HEADS_PER_STEP = 4
SCAN_CHUNKS = SUBLANES
SCAN_PAD = SUBLANES


def _gelu_tanh(x):
    half = 0.5 * x
    inner = x * (0.7978845608028654 + (0.7978845608028654 * 0.044715) * (x * x))
    return half + half * jnp.tanh(inner)


def _sigmoid(x):
    return 0.5 + 0.5 * jnp.tanh(0.5 * x)


def _sqrt_clamped(y):
    return jnp.where(y >= F32_MIN_NORMAL, y * lax.rsqrt(y), 0.0)


def _causal_conv(u, taps, bias, zero_pad):
    c = u * taps[CONV_WIDTH - 1:CONV_WIDTH, :]
    if zero_pad:
        row = lax.broadcasted_iota(jnp.int32, (u.shape[0], 1), 0)
    for s in range(1, CONV_WIDTH):
        shifted = pltpu.roll(u, s, 0)
        if zero_pad:
            shifted = jnp.where(row >= s, shifted, 0.0)
        c = c + shifted * taps[CONV_WIDTH - 1 - s:CONV_WIDTH - s, :]
    return c + bias


def _lru_kernel(u_ref, gate_ref, cw_ref, cb_ref, w_ref, b_ref, lam_ref, o_ref, c_s, a_s, b_s, h_s, p_s):
    t_len = u_ref.shape[1]
    heads = a_s.shape[0]
    chunk = t_len // SCAN_CHUNKS
    pitch = chunk + SCAN_PAD

    for hh in range(heads):
        lanes = slice(hh * LRU_BLOCK, (hh + 1) * LRU_BLOCK)
        p_s[hh, 0:SUBLANES, :] = jnp.zeros((SUBLANES, LRU_BLOCK), F32)
        p_s[hh, SUBLANES:SUBLANES + t_len, :] = u_ref[0, :, lanes].astype(F32)
        c = cb_ref[:, lanes]
        for k in range(CONV_WIDTH):
            start = SUBLANES - (CONV_WIDTH - 1) + k
            c = c + p_s[hh, start:start + t_len, :] * cw_ref[k:k + 1, lanes]
        c_s[:, lanes] = c

    lam = lam_ref[...]
    half_neg_sp = (-0.5 * LRU_C) * (jnp.maximum(-lam, 0.0) + jnp.log(1.0 + jnp.exp(-jnp.abs(lam))))

    def gates(ci, carry):
        rows = pl.ds(pl.multiple_of(ci * chunk, chunk), chunk)
        dst = pl.ds(pl.multiple_of(ci * pitch, SUBLANES), chunk)
        first = jnp.logical_and(lax.broadcasted_iota(jnp.int32, (chunk, 1), 0) == 0, ci == 0)
        for hh in range(heads):
            lanes = slice(hh * LRU_BLOCK, (hh + 1) * LRU_BLOCK)
            cc = c_s[rows, lanes]
            t = jnp.tanh(jnp.dot(cc.astype(BF16), w_ref[hh], preferred_element_type=F32) + b_ref[hh])
            half_sp = half_neg_sp[:, lanes]
            log_a = half_sp * t[:, :LRU_BLOCK] + half_sp
            half_c = 0.5 * cc
            gated_c = half_c * t[:, LRU_BLOCK:] + half_c
            a = jnp.exp(log_a)
            mult = _sqrt_clamped(-jnp.tanh(log_a) * (a * a + 1.0))
            mult = jnp.where(first, 1.0, mult)
            a_s[hh, dst, :] = a
            b_s[hh, dst, :] = mult * gated_c
        return carry

    lax.fori_loop(0, SCAN_CHUNKS, gates, 0)

    def scan(j, state):
        src = pl.ds(j, SCAN_CHUNKS, stride=pitch)
        dst = pl.ds(pl.multiple_of(j * SCAN_CHUNKS, SCAN_CHUNKS), SCAN_CHUNKS)
        new = []
        for hh in range(heads):
            h, p = state[hh]
            a = a_s[hh, src, :]
            h = a * h + b_s[hh, src, :]
            p = a * p
            h_s[hh, dst, :] = h
            p_s[hh, dst, :] = p
            new.append((h, p))
        return tuple(new)

    zeros = jnp.zeros((SCAN_CHUNKS, LRU_BLOCK), F32)
    ends = lax.fori_loop(0, chunk, scan, tuple((zeros, zeros + 1.0) for _ in range(heads)), unroll=8)

    h_in = []
    for hh in range(heads):
        h_end, p_end = ends[hh]
        rows_in = [jnp.zeros((1, LRU_BLOCK), F32)]
        for s in range(1, SCAN_CHUNKS):
            rows_in.append(h_end[s - 1:s, :] + p_end[s - 1:s, :] * rows_in[-1])
        h_in.append(jnp.concatenate(rows_in, axis=0))

    def patch(j, carry):
        src = pl.ds(pl.multiple_of(j * SCAN_CHUNKS, SCAN_CHUNKS), SCAN_CHUNKS)
        dst = pl.ds(j, SCAN_CHUNKS, stride=pitch)
        for hh in range(heads):
            b_s[hh, dst, :] = h_s[hh, src, :] + p_s[hh, src, :] * h_in[hh]
        return carry

    lax.fori_loop(0, chunk, patch, 0, unroll=8)

    def emit(ci, carry):
        rows = pl.ds(pl.multiple_of(ci * chunk, chunk), chunk)
        src = pl.ds(pl.multiple_of(ci * pitch, SUBLANES), chunk)
        for hh in range(heads):
            lanes = slice(hh * LRU_BLOCK, (hh + 1) * LRU_BLOCK)
            gate = gate_ref[0, rows, lanes].astype(F32)
            o_ref[0, rows, lanes] = (b_s[hh, src, :] * _gelu_tanh(gate)).astype(o_ref.dtype)
        return carry

    lax.fori_loop(0, SCAN_CHUNKS, emit, 0)


def _lru(proj3, conv_w, conv_b, w_cat, b_cat, lam, width):
    bsz, t_len, _ = proj3.shape
    hps = LRU_HEADS_PER_STEP
    lw = hps * LRU_BLOCK
    groups = width // lw
    scan_rows = SCAN_CHUNKS * (t_len // SCAN_CHUNKS + SCAN_PAD)
    return pl.pallas_call(
        _lru_kernel,
        out_shape=jax.ShapeDtypeStruct((bsz, t_len, width), BF16),
        grid=(bsz, groups),
        in_specs=[
            pl.BlockSpec((1, t_len, lw), lambda b, h: (b, 0, h)),
            pl.BlockSpec((1, t_len, lw), lambda b, h: (b, 0, groups + h)),
            pl.BlockSpec((CONV_WIDTH, lw), lambda b, h: (0, h)),
            pl.BlockSpec((1, lw), lambda b, h: (0, h)),
            pl.BlockSpec((hps, LRU_BLOCK, 2 * LRU_BLOCK), lambda b, h: (h, 0, 0)),
            pl.BlockSpec((hps, 1, 2 * LRU_BLOCK), lambda b, h: (h, 0, 0)),
            pl.BlockSpec((1, lw), lambda b, h: (0, h)),
        ],
        out_specs=pl.BlockSpec((1, t_len, lw), lambda b, h: (b, 0, h)),
        scratch_shapes=[pltpu.VMEM((t_len, lw), F32),
                        pltpu.VMEM((hps, scan_rows, LRU_BLOCK), F32),
                        pltpu.VMEM((hps, scan_rows, LRU_BLOCK), F32),
                        pltpu.VMEM((hps, t_len, LRU_BLOCK), F32),
                        pltpu.VMEM((hps, t_len + SUBLANES, LRU_BLOCK), F32)],
        compiler_params=_params(("arbitrary", "arbitrary"), 48),
        name="lru",
    )(proj3, proj3, conv_w, conv_b.reshape(1, width), w_cat, b_cat, lam.reshape(1, width))


SB_TILE = 256
SB_TILES_PER_STEP = 8
SB_BLOCKS_AHEAD = 2


def _pair_rms(x, g, seg):
    x2 = x * x
    hi = x2.astype(BF16)
    lo = (x2 - hi.astype(F32)).astype(BF16)
    ss = jnp.dot(jnp.concatenate([hi, lo], axis=1), seg, preferred_element_type=F32)
    return x * lax.rsqrt(ss * (1.0 / SB_HEAD_DIM) + NORM_EPS) * g


def _sb_kernel(q_ref, k_ref, v_ref, gq_ref, gk_ref, tri_ref, seg_ref, o_ref,
               k2_s, v2_s, qs_s, acc_s, car_s, live_s):
    t_len = k_ref.shape[1]
    qi = pl.program_id(2)
    first_head = lax.broadcasted_iota(jnp.int32, (1, LANES), 1) < SB_HEAD_DIM
    zero = jnp.zeros((), BF16)

    @pl.when(qi == 0)
    def _():
        gk = gk_ref[...]

        def body(c, carry):
            rows = pl.ds(pl.multiple_of(c * SB_TILE, SB_TILE), SB_TILE)
            dst0 = pl.ds(pl.multiple_of(c * 2 * SB_TILE, SB_TILE), SB_TILE)
            dst1 = pl.ds(pl.multiple_of(c * 2 * SB_TILE + SB_TILE, SB_TILE), SB_TILE)
            kn = _pair_rms(k_ref[0, rows, :].astype(F32), gk, seg_ref[...]).astype(BF16)
            vb = v_ref[0, rows, :]
            k2_s[dst0, :] = jnp.where(first_head, kn, zero)
            k2_s[dst1, :] = jnp.where(first_head, zero, kn)
            v2_s[dst0, :] = jnp.where(first_head, vb, zero)
            v2_s[dst1, :] = jnp.where(first_head, zero, vb)
            return carry

        lax.fori_loop(0, t_len // SB_TILE, body, 0, unroll=4)

    scale = SB_HEAD_DIM ** -0.5
    qs_s[...] = _pair_rms(q_ref[0].astype(F32), gq_ref[...], seg_ref[...]).astype(BF16) * jnp.asarray(scale, BF16)
    sign_bit = jnp.uint32(0x80000000)
    first_tile = qi * SB_TILES_PER_STEP

    def tile_rows(t):
        return slice(t * SB_TILE, (t + 1) * SB_TILE)

    def block_rows(kb):
        return pl.ds(pl.multiple_of(kb * 2 * SB_TILE, 2 * SB_TILE), 2 * SB_TILE)

    def causal_mask():
        return (lax.broadcasted_iota(jnp.int32, (SB_TILE, SB_TILE), 1)
                < lax.broadcasted_iota(jnp.int32, (SB_TILE, SB_TILE), 0))

    def logits(t, kb, causal):
        z = lax.dot_general(qs_s[tile_rows(t), :], k2_s[block_rows(kb), :], (((1,), (1,)), ((), ())),
                            preferred_element_type=F32)
        out = []
        for h in range(2):
            zh = z[:, h * SB_TILE:(h + 1) * SB_TILE]
            neg_abs = lax.bitcast_convert_type(lax.bitcast_convert_type(zh, jnp.uint32) | sign_bit, F32)
            ls = jnp.minimum(zh, 0.0) - jnp.log(1.0 + jnp.exp(neg_abs))
            l1m = ls - zh
            if causal is not None:
                l1m = jnp.where(causal, l1m, 0.0)
            out.append((ls, l1m))
        return out

    def weights(ls_l1m, causal):
        ws, sums = [], []
        for ls, l1m in ls_l1m:
            tail = jnp.dot(l1m.astype(BF16), tri_ref[...], preferred_element_type=F32)
            w = jnp.exp(ls + tail)
            if causal is not None:
                w = jnp.where(causal, w, 0.0)
            ws.append(w.astype(BF16))
            sums.append(tail[:, 0:1] + l1m[:, 0:1])
        return jnp.concatenate(ws, axis=1), sums

    def weighted_values(kb, w):
        return jnp.dot(w, v2_s[block_rows(kb), :], preferred_element_type=F32)

    def later_factor(carry0, carry1):
        return jnp.where(first_head, jnp.exp(carry0), jnp.exp(carry1))

    def still_live(carry0, carry1):
        return (jnp.max(jnp.maximum(carry0, carry1)) > EXP_ZERO_BELOW).astype(jnp.int32)

    def region(units):
        causal = causal_mask()
        lg = [[logits(t, kb, causal if d else None) for kb, d in blocks] for t, blocks in enumerate(units)]
        wt = [[weights(l, causal if d else None) for l, (_, d) in zip(lg[t], blocks)]
              for t, blocks in enumerate(units)]
        for t, blocks in enumerate(units):
            acc = carry0 = carry1 = None
            for (kb, _), (w, sums) in zip(blocks, wt[t]):
                pv = weighted_values(kb, w)
                if acc is None:
                    acc, carry0, carry1 = pv, sums[0], sums[1]
                else:
                    acc = acc + later_factor(carry0, carry1) * pv
                    carry0, carry1 = carry0 + sums[0], carry1 + sums[1]
            acc_s[tile_rows(t), :] = acc
            car_s[2 * t] = carry0
            car_s[2 * t + 1] = carry1
            live_s[t] = still_live(carry0, carry1)

    @pl.when(qi == 0)
    def _():
        region([[(first_tile + t - j, j == 0) for j in range(min(t + 1, SB_BLOCKS_AHEAD))]
                for t in range(SB_TILES_PER_STEP)])

    @pl.when(qi != 0)
    def _():
        region([[(first_tile + t - j, j == 0) for j in range(SB_BLOCKS_AHEAD)]
                for t in range(SB_TILES_PER_STEP)])

    def cond(state):
        kb, live = state
        return jnp.logical_and(kb >= 0, live > 0)

    for t in range(SB_TILES_PER_STEP):
        def body(state, t=t):
            kb, _ = state
            carry0, carry1 = car_s[2 * t], car_s[2 * t + 1]
            w, sums = weights(logits(t, kb, None), None)
            acc_s[tile_rows(t), :] += later_factor(carry0, carry1) * weighted_values(kb, w)
            carry0, carry1 = carry0 + sums[0], carry1 + sums[1]
            car_s[2 * t] = carry0
            car_s[2 * t + 1] = carry1
            return kb - 1, still_live(carry0, carry1)

        lax.while_loop(cond, body, (first_tile + t - SB_BLOCKS_AHEAD, live_s[t]))

    o_ref[0] = acc_s[...].astype(o_ref.dtype)


def _sb_attention(proj3, gq2, gk2, q_col, k_col, v_col, heads):
    bsz, t_len, _ = proj3.shape
    pairs = heads // 2
    lane_head = jnp.arange(LANES) // SB_HEAD_DIM
    same_head = (lane_head[:, None] == lane_head[None, :]).astype(BF16)
    seg2 = jnp.concatenate([same_head, same_head], axis=0)
    key = jnp.arange(SB_TILE)
    later_key = (key[:, None] > key[None, :]).astype(BF16)
    q_rows = SB_TILE * SB_TILES_PER_STEP
    return pl.pallas_call(
        _sb_kernel,
        out_shape=jax.ShapeDtypeStruct((bsz, t_len, heads * SB_HEAD_DIM), BF16),
        grid=(bsz, pairs, t_len // q_rows),
        in_specs=[
            pl.BlockSpec((1, q_rows, LANES), lambda b, p, i: (b, i, q_col // LANES + p)),
            pl.BlockSpec((1, t_len, LANES), lambda b, p, i: (b, 0, k_col // LANES + p)),
            pl.BlockSpec((1, t_len, LANES), lambda b, p, i: (b, 0, v_col // LANES + p)),
            pl.BlockSpec((1, LANES), lambda b, p, i: (0, 0)),
            pl.BlockSpec((1, LANES), lambda b, p, i: (0, 0)),
            pl.BlockSpec((SB_TILE, SB_TILE), lambda b, p, i: (0, 0)),
            pl.BlockSpec((2 * LANES, LANES), lambda b, p, i: (0, 0)),
        ],
        out_specs=pl.BlockSpec((1, q_rows, LANES), lambda b, p, i: (b, i, p)),
        scratch_shapes=[pltpu.VMEM((2 * t_len, LANES), BF16),
                        pltpu.VMEM((2 * t_len, LANES), BF16),
                        pltpu.VMEM((q_rows, LANES), BF16),
                        pltpu.VMEM((q_rows, LANES), F32),
                        pltpu.VMEM((2 * SB_TILES_PER_STEP, SB_TILE, 1), F32),
                        pltpu.SMEM((SB_TILES_PER_STEP,), jnp.int32)],
        compiler_params=_params(("arbitrary", "arbitrary", "arbitrary"), 32),
        name="sb_attn",
    )(proj3, proj3, proj3, gq2, gk2, later_key, seg2)


MEM_TILE = 512


def _mem_attn_kernel(q_ref, kv_ref, gq_ref, gk_ref, o_ref, kn_s):
    width = MEM_HEADS * MEM_HEAD_DIM

    @pl.when(pl.program_id(1) == 0)
    def _():
        for h in range(MEM_HEADS):
            cols = slice(h * MEM_HEAD_DIM, (h + 1) * MEM_HEAD_DIM)
            kn_s[:, cols] = _rms_rows(kv_ref[0, :, cols].astype(F32), gk_ref[...]).astype(BF16)

    scale = MEM_HEAD_DIM ** -0.5
    for h in range(MEM_HEADS):
        cols = slice(h * MEM_HEAD_DIM, (h + 1) * MEM_HEAD_DIM)
        qn = _rms_rows(q_ref[0, :, cols].astype(F32), gq_ref[...]).astype(BF16) * jnp.asarray(scale, BF16)
        s = lax.dot_general(qn, kn_s[:, cols], (((1,), (1,)), ((), ())), preferred_element_type=F32)
        e = jnp.exp(s - jnp.max(s, axis=-1, keepdims=True))
        denom = jnp.sum(e, axis=-1, keepdims=True)
        v = kv_ref[0, :, width + h * MEM_HEAD_DIM:width + (h + 1) * MEM_HEAD_DIM]
        o = jnp.dot(e.astype(BF16), v, preferred_element_type=F32) / denom
        o_ref[0, :, cols] = o.astype(o_ref.dtype)


def _mem_attention(proj3, memkv3, gq, gk, q_col):
    bsz, t_len, _ = proj3.shape
    m_len = memkv3.shape[1]
    width = MEM_HEADS * MEM_HEAD_DIM
    return pl.pallas_call(
        _mem_attn_kernel,
        out_shape=jax.ShapeDtypeStruct((bsz, t_len, width), BF16),
        grid=(bsz, t_len // MEM_TILE),
        in_specs=[
            pl.BlockSpec((1, MEM_TILE, width), lambda b, i: (b, i, q_col // width)),
            pl.BlockSpec((1, m_len, 2 * width), lambda b, i: (b, 0, 0)),
            pl.BlockSpec((1, MEM_HEAD_DIM), lambda b, i: (0, 0)),
            pl.BlockSpec((1, MEM_HEAD_DIM), lambda b, i: (0, 0)),
        ],
        out_specs=pl.BlockSpec((1, MEM_TILE, width), lambda b, i: (b, i, 0)),
        scratch_shapes=[pltpu.VMEM((m_len, width), BF16)],
        compiler_params=_params(("arbitrary", "arbitrary"), 32),
        name="mem_attn",
    )(proj3, memkv3, gq.reshape(1, MEM_HEAD_DIM), gk.reshape(1, MEM_HEAD_DIM))


MERGE_TM = 1024
MERGE_TN = 512


def _merge_kernel(ya_ref, yb_ref, yc_ref, wa_ref, wb_ref, wc_ref, g0_ref, g1_ref, g2_ref,
                  b0_ref, b1_ref, b2_ref, o_ref):
    out = None
    for y_ref, w_ref, g_ref, b_ref in ((ya_ref, wa_ref, g0_ref, b0_ref), (yb_ref, wb_ref, g1_ref, b1_ref),
                                       (yc_ref, wc_ref, g2_ref, b2_ref)):
        gate = _sigmoid(g_ref[...].astype(F32) + b_ref[...])
        term = gate * jnp.dot(y_ref[...], w_ref[...], preferred_element_type=F32)
        out = term if out is None else out + term
    o_ref[...] = out.astype(o_ref.dtype)


def _merge(ya, yb, yc, w_pa, w_pb, w_pc, proj2, b_gate, gate_col, d_model):
    n = ya.shape[0]
    tm, tn = MERGE_TM, MERGE_TN
    gate_blk = gate_col // tn
    per_branch = d_model // tn

    def y_spec(width):
        return pl.BlockSpec((tm, width), lambda i, j: (i, 0))

    def w_spec(width):
        return pl.BlockSpec((width, tn), lambda i, j: (0, j))

    def g_spec(branch):
        return pl.BlockSpec((tm, tn), lambda i, j: (i, gate_blk + branch * per_branch + j))

    def b_spec(branch):
        return pl.BlockSpec((1, tn), lambda i, j: (0, branch * per_branch + j))

    b_gate2 = b_gate.reshape(1, -1)
    return pl.pallas_call(
        _merge_kernel,
        out_shape=jax.ShapeDtypeStruct((n, d_model), BF16),
        grid=(n // tm, d_model // tn),
        in_specs=[y_spec(ya.shape[1]), y_spec(yb.shape[1]), y_spec(yc.shape[1]),
                  w_spec(w_pa.shape[0]), w_spec(w_pb.shape[0]), w_spec(w_pc.shape[0]),
                  g_spec(0), g_spec(1), g_spec(2), b_spec(0), b_spec(1), b_spec(2)],
        out_specs=pl.BlockSpec((tm, tn), lambda i, j: (i, j)),
        compiler_params=_params(("arbitrary", "arbitrary"), 48),
        name="merge",
    )(ya, yb, yc, w_pa, w_pb, w_pc, proj2, proj2, proj2, b_gate2, b_gate2, b_gate2)


OUT_TM = 512


def _out_proj_kernel(x_ref, m_ref, w_ref, g_ref, x1_ref, h2_ref):
    x1_ref[...] = x_ref[...] + jnp.dot(m_ref[...], w_ref[...], preferred_element_type=F32)
    g = g_ref[...]

    def body(c, carry):
        rows = pl.ds(pl.multiple_of(c * NORM_CHUNK, NORM_CHUNK), NORM_CHUNK)
        h2_ref[rows, :] = _rms_rows(x1_ref[rows, :], g).astype(BF16)
        return carry

    lax.fori_loop(0, x_ref.shape[0] // NORM_CHUNK, body, 0, unroll=2)


def _out_proj(x2, merged, w_out, g_ffn):
    n, d = x2.shape
    tm = OUT_TM
    row = pl.BlockSpec((tm, d), lambda i: (i, 0))
    return pl.pallas_call(
        _out_proj_kernel,
        out_shape=(jax.ShapeDtypeStruct((n, d), F32), jax.ShapeDtypeStruct((n, d), BF16)),
        grid=(n // tm,),
        in_specs=[row, row, pl.BlockSpec((d, d), lambda i: (0, 0)), pl.BlockSpec((1, d), lambda i: (0, 0))],
        out_specs=(row, row),
        compiler_params=_params(("arbitrary",), 48),
        name="out_proj",
    )(x2, merged, w_out, g_ffn.reshape(1, d))


FFN_TM = 1024
FFN_TF = 512
FFN_TN = 512


def _ffn_up_kernel(h_ref, wg_ref, wu_ref, o_ref):
    h = h_ref[...]
    gate = jnp.dot(h, wg_ref[...], preferred_element_type=F32)
    up = jnp.dot(h, wu_ref[...], preferred_element_type=F32)
    o_ref[...] = (gate * _sigmoid(gate) * up).astype(o_ref.dtype)


def _ffn_down_kernel(a_ref, w_ref, x1_ref, o_ref):
    o_ref[...] = x1_ref[...] + jnp.dot(a_ref[...], w_ref[...], preferred_element_type=F32)


def _ffn(h2, w_fc, w_down, x1):
    n, d = h2.shape
    d_ff = w_down.shape[0]
    tm, tf, tn = FFN_TM, FFN_TF, FFN_TN
    n_f = d_ff // tf
    act = pl.pallas_call(
        _ffn_up_kernel,
        out_shape=jax.ShapeDtypeStruct((n, d_ff), BF16),
        grid=(n // tm, n_f),
        in_specs=[pl.BlockSpec((tm, d), lambda i, f: (i, 0)),
                  pl.BlockSpec((d, tf), lambda i, f: (0, f)),
                  pl.BlockSpec((d, tf), lambda i, f: (0, n_f + f))],
        out_specs=pl.BlockSpec((tm, tf), lambda i, f: (i, f)),
        compiler_params=_params(("arbitrary", "arbitrary"), 40),
        name="ffn_up",
    )(h2, w_fc, w_fc)
    return pl.pallas_call(
        _ffn_down_kernel,
        out_shape=jax.ShapeDtypeStruct((n, d), F32),
        grid=(n // tm, d // tn),
        in_specs=[pl.BlockSpec((tm, d_ff), lambda i, j: (i, 0)),
                  pl.BlockSpec((d_ff, tn), lambda i, j: (0, j)),
                  pl.BlockSpec((tm, tn), lambda i, j: (i, j))],
        out_specs=pl.BlockSpec((tm, tn), lambda i, j: (i, j)),
        compiler_params=_params(("arbitrary", "arbitrary"), 56),
        name="ffn_down",
    )(act, w_down, x1)


def _layer(x, mem, g_mix, g_mem, w_in, b_gate, conv_w, conv_b, lru_wa, lru_ba, lru_wx, lru_bx,
           lru_lambda, sb_gq, sb_gk, mem_w_kv, mem_gq, mem_gk, w_pa, w_pb, w_pc, w_out,
           g_ffn, w_fc, w_down):
    bsz, t_len, d = x.shape
    m_len = mem.shape[1]
    n = bsz * t_len
    lru_width = w_pa.shape[0]
    sb_width = w_pb.shape[0]
    mem_width = w_pc.shape[0]
    in_cols = w_in.shape[1]
    q_sb_col = 2 * lru_width
    k_sb_col = q_sb_col + sb_width
    v_sb_col = k_sb_col + sb_width
    q_mem_col = v_sb_col + sb_width
    gate_col = q_mem_col + mem_width

    x2 = x.reshape(n, d)
    proj2 = _norm_matmul(x2, g_mix, w_in.astype(BF16), 1024, 2048, 56, "in_proj")
    proj3 = proj2.reshape(bsz, t_len, in_cols)

    w_cat = (0.5 * jnp.concatenate([lru_wa, lru_wx], axis=-1)).astype(BF16)
    b_cat = 0.5 * jnp.concatenate([lru_ba, lru_bx], axis=-1)[:, None, :]
    ya = _lru(proj3, conv_w, conv_b, w_cat, b_cat, lru_lambda, lru_width)

    yb = _sb_attention(proj3, jnp.tile(sb_gq, 2)[None, :], jnp.tile(sb_gk, 2)[None, :],
                       q_sb_col, k_sb_col, v_sb_col, sb_width // SB_HEAD_DIM)

    memkv = _norm_matmul(mem.reshape(bsz * m_len, d), g_mem, mem_w_kv.astype(BF16), 1024, 1024, 48, "mem_kv")
    yc = _mem_attention(proj3, memkv.reshape(bsz, m_len, 2 * mem_width), mem_gq, mem_gk, q_mem_col)

    merged = _merge(ya.reshape(n, lru_width), yb.reshape(n, sb_width), yc.reshape(n, mem_width),
                    w_pa.astype(BF16), w_pb.astype(BF16), w_pc.astype(BF16), proj2, b_gate, gate_col, d)
    x1, h2 = _out_proj(x2, merged, w_out.astype(BF16), g_ffn)
    out = _ffn(h2, w_fc.astype(BF16), w_down.astype(BF16), x1)
    return out.reshape(bsz, t_len, d)


def kernel(x, mem, g_mix, g_mem, w_in, b_gate, conv_w, conv_b, lru_wa, lru_ba, lru_wx, lru_bx,
           lru_lambda, sb_gq, sb_gk, mem_w_kv, mem_gq, mem_gk, w_pa, w_pb, w_pc, w_out,
           g_ffn, w_fc, w_down):
    for l in range(g_mix.shape[0]):
        x = _layer(x, mem, g_mix[l], g_mem[l], w_in[l], b_gate[l], conv_w[l], conv_b[l], lru_wa[l],
                   lru_ba[l], lru_wx[l], lru_bx[l], lru_lambda[l], sb_gq[l], sb_gk[l], mem_w_kv[l],
                   mem_gq[l], mem_gk[l], w_pa[l], w_pb[l], w_pc[l], w_out[l], g_ffn[l], w_fc[l],
                   w_down[l])
    return x
```

```python
import jax
import jax.numpy as jnp
from jax import lax
from jax.experimental import pallas as pl
from jax.experimental.pallas import tpu as pltpu

F32 = jnp.float32
BF16 = jnp.bfloat16

NORM_EPS = 1e-6
LRU_C = 8.0
CONV_WIDTH = 4
LRU_BLOCK = 128
SB_HEAD_DIM = 64
MEM_HEAD_DIM = 256
MEM_HEADS = 4

LANES = 128
SUBLANES = 8
MIB = 1024 * 1024

F32_MIN_NORMAL = 1.1754943508222875e-38

EXP_ZERO_BELOW = -104.0


def _params(semantics, vmem_mib):
    return pltpu.CompilerParams(dimension_semantics=semantics, vmem_limit_bytes=vmem_mib * MIB)


NORM_CHUNK = 64


def _rms_rows(x, g):
    ms = jnp.mean(x * x, axis=-1, keepdims=True)
    return x * lax.rsqrt(ms + NORM_EPS) * g


def _norm_matmul_kernel(x_ref, g_ref, w_ref, o_ref, h_ref):
    @pl.when(pl.program_id(1) == 0)
    def _():
        g = g_ref[...]

        def body(c, carry):
            rows = pl.ds(pl.multiple_of(c * NORM_CHUNK, NORM_CHUNK), NORM_CHUNK)
            h_ref[rows, :] = _rms_rows(x_ref[rows, :].astype(F32), g).astype(BF16)
            return carry

        lax.fori_loop(0, x_ref.shape[0] // NORM_CHUNK, body, 0, unroll=2)

    o_ref[...] = jnp.dot(h_ref[...], w_ref[...], preferred_element_type=F32).astype(o_ref.dtype)


def _norm_matmul(x2d, g, w, tm, tn, vmem_mib, name):
    m, k = x2d.shape
    n = w.shape[1]
    return pl.pallas_call(
        _norm_matmul_kernel,
        out_shape=jax.ShapeDtypeStruct((m, n), BF16),
        grid=(m // tm, n // tn),
        in_specs=[
            pl.BlockSpec((tm, k), lambda i, j: (i, 0)),
            pl.BlockSpec((1, k), lambda i, j: (0, 0)),
            pl.BlockSpec((k, tn), lambda i, j: (0, j)),
        ],
        out_specs=pl.BlockSpec((tm, tn), lambda i, j: (i, j)),
        scratch_shapes=[pltpu.VMEM((tm, k), BF16)],
        compiler_params=_params(("arbitrary", "arbitrary"), vmem_mib),
        name=name,
    )(x2d, g.reshape(1, k), w)


LRU_HEADS_PER_STEP = 4
SCAN_CHUNKS = SUBLANES
SCAN_PAD = SUBLANES


def _gelu_tanh(x):
    half = 0.5 * x
    inner = x * (0.7978845608028654 + (0.7978845608028654 * 0.044715) * (x * x))
    return half + half * jnp.tanh(inner)


def _sigmoid(x):
    return 0.5 + 0.5 * jnp.tanh(0.5 * x)


def _sqrt_clamped(y):
    return jnp.where(y >= F32_MIN_NORMAL, y * lax.rsqrt(y), 0.0)


def _causal_conv(u, taps, bias, zero_pad):
    c = u * taps[CONV_WIDTH - 1:CONV_WIDTH, :]
    if zero_pad:
        row = lax.broadcasted_iota(jnp.int32, (u.shape[0], 1), 0)
    for s in range(1, CONV_WIDTH):
        shifted = pltpu.roll(u, s, 0)
        if zero_pad:
            shifted = jnp.where(row >= s, shifted, 0.0)
        c = c + shifted * taps[CONV_WIDTH - 1 - s:CONV_WIDTH - s, :]
    return c + bias


def _lru_kernel(u_ref, gate_ref, cw_ref, cb_ref, w_ref, b_ref, lam_ref, o_ref, c_s, a_s, b_s, h_s, p_s):
    t_len = u_ref.shape[1]
    heads = a_s.shape[0]
    chunk = t_len // SCAN_CHUNKS
    pitch = chunk + SCAN_PAD

    for hh in range(heads):
        lanes = slice(hh * LRU_BLOCK, (hh + 1) * LRU_BLOCK)
        p_s[hh, 0:SUBLANES, :] = jnp.zeros((SUBLANES, LRU_BLOCK), F32)
        p_s[hh, SUBLANES:SUBLANES + t_len, :] = u_ref[0, :, lanes].astype(F32)
        c = cb_ref[:, lanes]
        for k in range(CONV_WIDTH):
            start = SUBLANES - (CONV_WIDTH - 1) + k
            c = c + p_s[hh, start:start + t_len, :] * cw_ref[k:k + 1, lanes]
        c_s[:, lanes] = c

    lam = lam_ref[...]
    half_neg_sp = (-0.5 * LRU_C) * (jnp.maximum(-lam, 0.0) + jnp.log(1.0 + jnp.exp(-jnp.abs(lam))))

    def gates(ci, carry):
        rows = pl.ds(pl.multiple_of(ci * chunk, chunk), chunk)
        dst = pl.ds(pl.multiple_of(ci * pitch, SUBLANES), chunk)
        first = jnp.logical_and(lax.broadcasted_iota(jnp.int32, (chunk, 1), 0) == 0, ci == 0)
        for hh in range(heads):
            lanes = slice(hh * LRU_BLOCK, (hh + 1) * LRU_BLOCK)
            cc = c_s[rows, lanes]
            t = jnp.tanh(jnp.dot(cc.astype(BF16), w_ref[hh], preferred_element_type=F32) + b_ref[hh])
            half_sp = half_neg_sp[:, lanes]
            log_a = half_sp * t[:, :LRU_BLOCK] + half_sp
            half_c = 0.5 * cc
            gated_c = half_c * t[:, LRU_BLOCK:] + half_c
            a = jnp.exp(log_a)
            mult = _sqrt_clamped(-jnp.tanh(log_a) * (a * a + 1.0))
            mult = jnp.where(first, 1.0, mult)
            a_s[hh, dst, :] = a
            b_s[hh, dst, :] = mult * gated_c
        return carry

    lax.fori_loop(0, SCAN_CHUNKS, gates, 0)

    def scan(j, state):
        src = pl.ds(j, SCAN_CHUNKS, stride=pitch)
        dst = pl.ds(pl.multiple_of(j * SCAN_CHUNKS, SCAN_CHUNKS), SCAN_CHUNKS)
        new = []
        for hh in range(heads):
            h, p = state[hh]
            a = a_s[hh, src, :]
            h = a * h + b_s[hh, src, :]
            p = a * p
            h_s[hh, dst, :] = h
            p_s[hh, dst, :] = p
            new.append((h, p))
        return tuple(new)

    zeros = jnp.zeros((SCAN_CHUNKS, LRU_BLOCK), F32)
    ends = lax.fori_loop(0, chunk, scan, tuple((zeros, zeros + 1.0) for _ in range(heads)), unroll=8)

    h_in = []
    for hh in range(heads):
        h_end, p_end = ends[hh]
        rows_in = [jnp.zeros((1, LRU_BLOCK), F32)]
        for s in range(1, SCAN_CHUNKS):
            rows_in.append(h_end[s - 1:s, :] + p_end[s - 1:s, :] * rows_in[-1])
        h_in.append(jnp.concatenate(rows_in, axis=0))

    def patch(j, carry):
        src = pl.ds(pl.multiple_of(j * SCAN_CHUNKS, SCAN_CHUNKS), SCAN_CHUNKS)
        dst = pl.ds(j, SCAN_CHUNKS, stride=pitch)
        for hh in range(heads):
            b_s[hh, dst, :] = h_s[hh, src, :] + p_s[hh, src, :] * h_in[hh]
        return carry

    lax.fori_loop(0, chunk, patch, 0, unroll=8)

    def emit(ci, carry):
        rows = pl.ds(pl.multiple_of(ci * chunk, chunk), chunk)
        src = pl.ds(pl.multiple_of(ci * pitch, SUBLANES), chunk)
        for hh in range(heads):
            lanes = slice(hh * LRU_BLOCK, (hh + 1) * LRU_BLOCK)
            gate = gate_ref[0, rows, lanes].astype(F32)
            o_ref[0, rows, lanes] = (b_s[hh, src, :] * _gelu_tanh(gate)).astype(o_ref.dtype)
        return carry

    lax.fori_loop(0, SCAN_CHUNKS, emit, 0)


def _lru(proj3, conv_w, conv_b, w_cat, b_cat, lam, width):
    bsz, t_len, _ = proj3.shape
    hps = LRU_HEADS_PER_STEP
    lw = hps * LRU_BLOCK
    groups = width // lw
    scan_rows = SCAN_CHUNKS * (t_len // SCAN_CHUNKS + SCAN_PAD)
    return pl.pallas_call(
        _lru_kernel,
        out_shape=jax.ShapeDtypeStruct((bsz, t_len, width), BF16),
        grid=(bsz, groups),
        in_specs=[
            pl.BlockSpec((1, t_len, lw), lambda b, h: (b, 0, h)),
            pl.BlockSpec((1, t_len, lw), lambda b, h: (b, 0, groups + h)),
            pl.BlockSpec((CONV_WIDTH, lw), lambda b, h: (0, h)),
            pl.BlockSpec((1, lw), lambda b, h: (0, h)),
            pl.BlockSpec((hps, LRU_BLOCK, 2 * LRU_BLOCK), lambda b, h: (h, 0, 0)),
            pl.BlockSpec((hps, 1, 2 * LRU_BLOCK), lambda b, h: (h, 0, 0)),
            pl.BlockSpec((1, lw), lambda b, h: (0, h)),
        ],
        out_specs=pl.BlockSpec((1, t_len, lw), lambda b, h: (b, 0, h)),
        scratch_shapes=[pltpu.VMEM((t_len, lw), F32),
                        pltpu.VMEM((hps, scan_rows, LRU_BLOCK), F32),
                        pltpu.VMEM((hps, scan_rows, LRU_BLOCK), F32),
                        pltpu.VMEM((hps, t_len, LRU_BLOCK), F32),
                        pltpu.VMEM((hps, t_len + SUBLANES, LRU_BLOCK), F32)],
        compiler_params=_params(("arbitrary", "arbitrary"), 48),
        name="lru",
    )(proj3, proj3, conv_w, conv_b.reshape(1, width), w_cat, b_cat, lam.reshape(1, width))


SB_TILE = 256
SB_TILES_PER_STEP = 8
SB_BLOCKS_AHEAD = 2


def _pair_rms(x, g, seg):
    x2 = x * x
    hi = x2.astype(BF16)
    lo = (x2 - hi.astype(F32)).astype(BF16)
    ss = jnp.dot(jnp.concatenate([hi, lo], axis=1), seg, preferred_element_type=F32)
    return x * lax.rsqrt(ss * (1.0 / SB_HEAD_DIM) + NORM_EPS) * g


def _sb_kernel(q_ref, k_ref, v_ref, gq_ref, gk_ref, tri_ref, seg_ref, o_ref,
               k2_s, v2_s, qs_s, acc_s, car_s, live_s):
    t_len = k_ref.shape[1]
    qi = pl.program_id(2)
    first_head = lax.broadcasted_iota(jnp.int32, (1, LANES), 1) < SB_HEAD_DIM
    zero = jnp.zeros((), BF16)

    @pl.when(qi == 0)
    def _():
        gk = gk_ref[...]

        def body(c, carry):
            rows = pl.ds(pl.multiple_of(c * SB_TILE, SB_TILE), SB_TILE)
            dst0 = pl.ds(pl.multiple_of(c * 2 * SB_TILE, SB_TILE), SB_TILE)
            dst1 = pl.ds(pl.multiple_of(c * 2 * SB_TILE + SB_TILE, SB_TILE), SB_TILE)
            kn = _pair_rms(k_ref[0, rows, :].astype(F32), gk, seg_ref[...]).astype(BF16)
            vb = v_ref[0, rows, :]
            k2_s[dst0, :] = jnp.where(first_head, kn, zero)
            k2_s[dst1, :] = jnp.where(first_head, zero, kn)
            v2_s[dst0, :] = jnp.where(first_head, vb, zero)
            v2_s[dst1, :] = jnp.where(first_head, zero, vb)
            return carry

        lax.fori_loop(0, t_len // SB_TILE, body, 0, unroll=4)

    scale = SB_HEAD_DIM ** -0.5
    qs_s[...] = _pair_rms(q_ref[0].astype(F32), gq_ref[...], seg_ref[...]).astype(BF16) * jnp.asarray(scale, BF16)
    sign_bit = jnp.uint32(0x80000000)
    first_tile = qi * SB_TILES_PER_STEP

    def tile_rows(t):
        return slice(t * SB_TILE, (t + 1) * SB_TILE)

    def block_rows(kb):
        return pl.ds(pl.multiple_of(kb * 2 * SB_TILE, 2 * SB_TILE), 2 * SB_TILE)

    def causal_mask():
        return (lax.broadcasted_iota(jnp.int32, (SB_TILE, SB_TILE), 1)
                < lax.broadcasted_iota(jnp.int32, (SB_TILE, SB_TILE), 0))

    def logits(t, kb, causal):
        z = lax.dot_general(qs_s[tile_rows(t), :], k2_s[block_rows(kb), :], (((1,), (1,)), ((), ())),
                            preferred_element_type=F32)
        out = []
        for h in range(2):
            zh = z[:, h * SB_TILE:(h + 1) * SB_TILE]
            neg_abs = lax.bitcast_convert_type(lax.bitcast_convert_type(zh, jnp.uint32) | sign_bit, F32)
            ls = jnp.minimum(zh, 0.0) - jnp.log(1.0 + jnp.exp(neg_abs))
            l1m = ls - zh
            if causal is not None:
                l1m = jnp.where(causal, l1m, 0.0)
            out.append((ls, l1m))
        return out

    def weights(ls_l1m, causal):
        ws, sums = [], []
        for ls, l1m in ls_l1m:
            tail = jnp.dot(l1m.astype(BF16), tri_ref[...], preferred_element_type=F32)
            w = jnp.exp(ls + tail)
            if causal is not None:
                w = jnp.where(causal, w, 0.0)
            ws.append(w.astype(BF16))
            sums.append(tail[:, 0:1] + l1m[:, 0:1])
        return jnp.concatenate(ws, axis=1), sums

    def weighted_values(kb, w):
        return jnp.dot(w, v2_s[block_rows(kb), :], preferred_element_type=F32)

    def later_factor(carry0, carry1):
        return jnp.where(first_head, jnp.exp(carry0), jnp.exp(carry1))

    def still_live(carry0, carry1):
        return (jnp.max(jnp.maximum(carry0, carry1)) > EXP_ZERO_BELOW).astype(jnp.int32)

    def region(units):
        causal = causal_mask()
        lg = [[logits(t, kb, causal if d else None) for kb, d in blocks] for t, blocks in enumerate(units)]
        wt = [[weights(l, causal if d else None) for l, (_, d) in zip(lg[t], blocks)]
              for t, blocks in enumerate(units)]
        for t, blocks in enumerate(units):
            acc = carry0 = carry1 = None
            for (kb, _), (w, sums) in zip(blocks, wt[t]):
                pv = weighted_values(kb, w)
                if acc is None:
                    acc, carry0, carry1 = pv, sums[0], sums[1]
                else:
                    acc = acc + later_factor(carry0, carry1) * pv
                    carry0, carry1 = carry0 + sums[0], carry1 + sums[1]
            acc_s[tile_rows(t), :] = acc
            car_s[2 * t] = carry0
            car_s[2 * t + 1] = carry1
            live_s[t] = still_live(carry0, carry1)

    @pl.when(qi == 0)
    def _():
        region([[(first_tile + t - j, j == 0) for j in range(min(t + 1, SB_BLOCKS_AHEAD))]
                for t in range(SB_TILES_PER_STEP)])

    @pl.when(qi != 0)
    def _():
        region([[(first_tile + t - j, j == 0) for j in range(SB_BLOCKS_AHEAD)]
                for t in range(SB_TILES_PER_STEP)])

    def cond(state):
        kb, live = state
        return jnp.logical_and(kb >= 0, live > 0)

    for t in range(SB_TILES_PER_STEP):
        def body(state, t=t):
            kb, _ = state
            carry0, carry1 = car_s[2 * t], car_s[2 * t + 1]
            w, sums = weights(logits(t, kb, None), None)
            acc_s[tile_rows(t), :] += later_factor(carry0, carry1) * weighted_values(kb, w)
            carry0, carry1 = carry0 + sums[0], carry1 + sums[1]
            car_s[2 * t] = carry0
            car_s[2 * t + 1] = carry1
            return kb - 1, still_live(carry0, carry1)

        lax.while_loop(cond, body, (first_tile + t - SB_BLOCKS_AHEAD, live_s[t]))

    o_ref[0] = acc_s[...].astype(o_ref.dtype)


def _sb_attention(proj3, gq2, gk2, q_col, k_col, v_col, heads):
    bsz, t_len, _ = proj3.shape
    pairs = heads // 2
    lane_head = jnp.arange(LANES) // SB_HEAD_DIM
    same_head = (lane_head[:, None] == lane_head[None, :]).astype(BF16)
    seg2 = jnp.concatenate([same_head, same_head], axis=0)
    key = jnp.arange(SB_TILE)
    later_key = (key[:, None] > key[None, :]).astype(BF16)
    q_rows = SB_TILE * SB_TILES_PER_STEP
    return pl.pallas_call(
        _sb_kernel,
        out_shape=jax.ShapeDtypeStruct((bsz, t_len, heads * SB_HEAD_DIM), BF16),
        grid=(bsz, pairs, t_len // q_rows),
        in_specs=[
            pl.BlockSpec((1, q_rows, LANES), lambda b, p, i: (b, i, q_col // LANES + p)),
            pl.BlockSpec((1, t_len, LANES), lambda b, p, i: (b, 0, k_col // LANES + p)),
            pl.BlockSpec((1, t_len, LANES), lambda b, p, i: (b, 0, v_col // LANES + p)),
            pl.BlockSpec((1, LANES), lambda b, p, i: (0, 0)),
            pl.BlockSpec((1, LANES), lambda b, p, i: (0, 0)),
            pl.BlockSpec((SB_TILE, SB_TILE), lambda b, p, i: (0, 0)),
            pl.BlockSpec((2 * LANES, LANES), lambda b, p, i: (0, 0)),
        ],
        out_specs=pl.BlockSpec((1, q_rows, LANES), lambda b, p, i: (b, i, p)),
        scratch_shapes=[pltpu.VMEM((2 * t_len, LANES), BF16),
                        pltpu.VMEM((2 * t_len, LANES), BF16),
                        pltpu.VMEM((q_rows, LANES), BF16),
                        pltpu.VMEM((q_rows, LANES), F32),
                        pltpu.VMEM((2 * SB_TILES_PER_STEP, SB_TILE, 1), F32),
                        pltpu.SMEM((SB_TILES_PER_STEP,), jnp.int32)],
        compiler_params=_params(("arbitrary", "arbitrary", "arbitrary"), 32),
        name="sb_attn",
    )(proj3, proj3, proj3, gq2, gk2, later_key, seg2)


MEM_TILE = 512


def _mem_attn_kernel(q_ref, kv_ref, gq_ref, gk_ref, o_ref, kn_s):
    width = MEM_HEADS * MEM_HEAD_DIM

    @pl.when(pl.program_id(1) == 0)
    def _():
        for h in range(MEM_HEADS):
            cols = slice(h * MEM_HEAD_DIM, (h + 1) * MEM_HEAD_DIM)
            kn_s[:, cols] = _rms_rows(kv_ref[0, :, cols].astype(F32), gk_ref[...]).astype(BF16)

    scale = MEM_HEAD_DIM ** -0.5
    for h in range(MEM_HEADS):
        cols = slice(h * MEM_HEAD_DIM, (h + 1) * MEM_HEAD_DIM)
        qn = _rms_rows(q_ref[0, :, cols].astype(F32), gq_ref[...]).astype(BF16) * jnp.asarray(scale, BF16)
        s = lax.dot_general(qn, kn_s[:, cols], (((1,), (1,)), ((), ())), preferred_element_type=F32)
        e = jnp.exp(s - jnp.max(s, axis=-1, keepdims=True))
        denom = jnp.sum(e, axis=-1, keepdims=True)
        v = kv_ref[0, :, width + h * MEM_HEAD_DIM:width + (h + 1) * MEM_HEAD_DIM]
        o = jnp.dot(e.astype(BF16), v, preferred_element_type=F32) / denom
        o_ref[0, :, cols] = o.astype(o_ref.dtype)


def _mem_attention(proj3, memkv3, gq, gk, q_col):
    bsz, t_len, _ = proj3.shape
    m_len = memkv3.shape[1]
    width = MEM_HEADS * MEM_HEAD_DIM
    return pl.pallas_call(
        _mem_attn_kernel,
        out_shape=jax.ShapeDtypeStruct((bsz, t_len, width), BF16),
        grid=(bsz, t_len // MEM_TILE),
        in_specs=[
            pl.BlockSpec((1, MEM_TILE, width), lambda b, i: (b, i, q_col // width)),
            pl.BlockSpec((1, m_len, 2 * width), lambda b, i: (b, 0, 0)),
            pl.BlockSpec((1, MEM_HEAD_DIM), lambda b, i: (0, 0)),
            pl.BlockSpec((1, MEM_HEAD_DIM), lambda b, i: (0, 0)),
        ],
        out_specs=pl.BlockSpec((1, MEM_TILE, width), lambda b, i: (b, i, 0)),
        scratch_shapes=[pltpu.VMEM((m_len, width), BF16)],
        compiler_params=_params(("arbitrary", "arbitrary"), 32),
        name="mem_attn",
    )(proj3, memkv3, gq.reshape(1, MEM_HEAD_DIM), gk.reshape(1, MEM_HEAD_DIM))


MERGE_TM = 1024
MERGE_TN = 512


def _merge_kernel(ya_ref, yb_ref, yc_ref, wa_ref, wb_ref, wc_ref, g0_ref, g1_ref, g2_ref,
                  b0_ref, b1_ref, b2_ref, o_ref):
    tn = o_ref.shape[1]
    cols = pl.ds(pl.multiple_of(pl.program_id(1) * tn, tn), tn)
    out = None
    for y_ref, w_ref, g_ref, b_ref in ((ya_ref, wa_ref, g0_ref, b0_ref), (yb_ref, wb_ref, g1_ref, b1_ref),
                                       (yc_ref, wc_ref, g2_ref, b2_ref)):
        gate = _sigmoid(g_ref[...].astype(F32) + b_ref[...])
        term = gate * jnp.dot(y_ref[...], w_ref[:, cols], preferred_element_type=F32)
        out = term if out is None else out + term
    o_ref[...] = out.astype(o_ref.dtype)


def _merge(ya, yb, yc, w_pa, w_pb, w_pc, proj2, b_gate, gate_col, d_model):
    n = ya.shape[0]
    tm, tn = MERGE_TM, MERGE_TN
    gate_blk = gate_col // tn
    per_branch = d_model // tn

    def y_spec(width):
        return pl.BlockSpec((tm, width), lambda i, j: (i, 0))

    def w_spec(width):
        return pl.BlockSpec((width, d_model), lambda i, j: (0, 0), pipeline_mode=pl.Buffered(1))

    def g_spec(branch):
        return pl.BlockSpec((tm, tn), lambda i, j: (i, gate_blk + branch * per_branch + j))

    def b_spec(branch):
        return pl.BlockSpec((1, tn), lambda i, j: (0, branch * per_branch + j))

    b_gate2 = b_gate.reshape(1, -1)
    return pl.pallas_call(
        _merge_kernel,
        out_shape=jax.ShapeDtypeStruct((n, d_model), BF16),
        grid=(n // tm, d_model // tn),
        in_specs=[y_spec(ya.shape[1]), y_spec(yb.shape[1]), y_spec(yc.shape[1]),
                  w_spec(w_pa.shape[0]), w_spec(w_pb.shape[0]), w_spec(w_pc.shape[0]),
                  g_spec(0), g_spec(1), g_spec(2), b_spec(0), b_spec(1), b_spec(2)],
        out_specs=pl.BlockSpec((tm, tn), lambda i, j: (i, j)),
        compiler_params=_params(("arbitrary", "arbitrary"), 48),
        name="merge",
    )(ya, yb, yc, w_pa, w_pb, w_pc, proj2, proj2, proj2, b_gate2, b_gate2, b_gate2)


OUT_TM = 512


def _out_proj_kernel(x_ref, m_ref, w_ref, x1_ref):
    x1_ref[...] = x_ref[...] + jnp.dot(m_ref[...], w_ref[...], preferred_element_type=F32)


def _out_proj(x2, merged, w_out):
    n, d = x2.shape
    tm = OUT_TM
    row = pl.BlockSpec((tm, d), lambda i: (i, 0))
    return pl.pallas_call(
        _out_proj_kernel,
        out_shape=jax.ShapeDtypeStruct((n, d), F32),
        grid=(n // tm,),
        in_specs=[row, row, pl.BlockSpec((d, d), lambda i: (0, 0))],
        out_specs=row,
        compiler_params=_params(("arbitrary",), 40),
        name="out_proj",
    )(x2, merged, w_out)


FFN_TM = 1024
FFN_TF = 512
FFN_TN = 512
FFN_NORM_ROWS = 112


def _ffn_up_kernel(x1_ref, g_ref, wg_ref, wu_ref, o_ref, h_s):
    i, f = pl.program_id(0), pl.program_id(1)
    tm = x1_ref.shape[0]
    slot = i % 2

    @pl.when(jnp.logical_and(i == 0, f == 0))
    def _():
        def body(c, carry):
            rows = pl.ds(pl.multiple_of(c * NORM_CHUNK, NORM_CHUNK), NORM_CHUNK)
            h_s[0, rows, :] = _rms_rows(x1_ref[rows, :], g_ref[...]).astype(BF16)
            return carry

        lax.fori_loop(0, tm // NORM_CHUNK, body, 0, unroll=2)

    h = h_s[slot]
    gate = jnp.dot(h, wg_ref[...], preferred_element_type=F32)
    up = jnp.dot(h, wu_ref[...], preferred_element_type=F32)
    o_ref[...] = (gate * _sigmoid(gate) * up).astype(o_ref.dtype)

    start = jnp.clip((f - 1) * FFN_NORM_ROWS, 0, tm - FFN_NORM_ROWS)
    rows = pl.ds(pl.multiple_of(start, 2 * SUBLANES), FFN_NORM_ROWS)
    h_s[1 - slot, rows, :] = _rms_rows(x1_ref[rows, :], g_ref[...]).astype(BF16)


def _ffn_down_kernel(a_ref, w_ref, x1_ref, o_ref):
    tn = o_ref.shape[1]
    cols = pl.ds(pl.multiple_of(pl.program_id(1) * tn, tn), tn)
    o_ref[...] = x1_ref[...] + jnp.dot(a_ref[...], w_ref[:, cols], preferred_element_type=F32)


def _ffn(x1, g_ffn, w_fc, w_down):
    n, d = x1.shape
    d_ff = w_down.shape[0]
    tm, tf, tn = FFN_TM, FFN_TF, FFN_TN
    n_f = d_ff // tf
    last = n // tm - 1
    assert (n_f - 1) * FFN_NORM_ROWS >= tm and FFN_NORM_ROWS % (2 * SUBLANES) == 0 and tm % (2 * SUBLANES) == 0

    def row_tile_to_normalise(i, f):
        return jnp.where(jnp.logical_and(i == 0, f == 0), 0, jnp.minimum(i + 1, last)), 0

    act = pl.pallas_call(
        _ffn_up_kernel,
        out_shape=jax.ShapeDtypeStruct((n, d_ff), BF16),
        grid=(n // tm, n_f),
        in_specs=[pl.BlockSpec((tm, d), row_tile_to_normalise),
                  pl.BlockSpec((1, d), lambda i, f: (0, 0)),
                  pl.BlockSpec((d, tf), lambda i, f: (0, f)),
                  pl.BlockSpec((d, tf), lambda i, f: (0, n_f + f))],
        out_specs=pl.BlockSpec((tm, tf), lambda i, f: (i, f)),
        scratch_shapes=[pltpu.VMEM((2, tm, d), BF16)],
        compiler_params=_params(("arbitrary", "arbitrary"), 48),
        name="ffn_up",
    )(x1, g_ffn.reshape(1, d), w_fc, w_fc)
    return pl.pallas_call(
        _ffn_down_kernel,
        out_shape=jax.ShapeDtypeStruct((n, d), F32),
        grid=(n // tm, d // tn),
        in_specs=[pl.BlockSpec((tm, d_ff), lambda i, j: (i, 0)),
                  pl.BlockSpec((d_ff, d), lambda i, j: (0, 0), pipeline_mode=pl.Buffered(1)),
                  pl.BlockSpec((tm, tn), lambda i, j: (i, j))],
        out_specs=pl.BlockSpec((tm, tn), lambda i, j: (i, j)),
        compiler_params=_params(("arbitrary", "arbitrary"), 60),
        name="ffn_down",
    )(act, w_down, x1)


def _layer(x, mem, g_mix, g_mem, w_in, b_gate, conv_w, conv_b, lru_wa, lru_ba, lru_wx, lru_bx,
           lru_lambda, sb_gq, sb_gk, mem_w_kv, mem_gq, mem_gk, w_pa, w_pb, w_pc, w_out,
           g_ffn, w_fc, w_down):
    bsz, t_len, d = x.shape
    m_len = mem.shape[1]
    n = bsz * t_len
    lru_width = w_pa.shape[0]
    sb_width = w_pb.shape[0]
    mem_width = w_pc.shape[0]
    in_cols = w_in.shape[1]
    q_sb_col = 2 * lru_width
    k_sb_col = q_sb_col + sb_width
    v_sb_col = k_sb_col + sb_width
    q_mem_col = v_sb_col + sb_width
    gate_col = q_mem_col + mem_width

    x2 = x.reshape(n, d)
    proj2 = _norm_matmul(x2, g_mix, w_in.astype(BF16), 1024, 2048, 56, "in_proj")
    proj3 = proj2.reshape(bsz, t_len, in_cols)

    w_cat = (0.5 * jnp.concatenate([lru_wa, lru_wx], axis=-1)).astype(BF16)
    b_cat = 0.5 * jnp.concatenate([lru_ba, lru_bx], axis=-1)[:, None, :]
    ya = _lru(proj3, conv_w, conv_b, w_cat, b_cat, lru_lambda, lru_width)

    yb = _sb_attention(proj3, jnp.tile(sb_gq, 2)[None, :], jnp.tile(sb_gk, 2)[None, :],
                       q_sb_col, k_sb_col, v_sb_col, sb_width // SB_HEAD_DIM)

    memkv = _norm_matmul(mem.reshape(bsz * m_len, d), g_mem, mem_w_kv.astype(BF16), 1024, 1024, 48, "mem_kv")
    yc = _mem_attention(proj3, memkv.reshape(bsz, m_len, 2 * mem_width), mem_gq, mem_gk, q_mem_col)

    merged = _merge(ya.reshape(n, lru_width), yb.reshape(n, sb_width), yc.reshape(n, mem_width),
                    w_pa.astype(BF16), w_pb.astype(BF16), w_pc.astype(BF16), proj2, b_gate, gate_col, d)
    x1 = _out_proj(x2, merged, w_out.astype(BF16))
    out = _ffn(x1, g_ffn, w_fc.astype(BF16), w_down.astype(BF16))
    return out.reshape(bsz, t_len, d)


def kernel(x, mem, g_mix, g_mem, w_in, b_gate, conv_w, conv_b, lru_wa, lru_ba, lru_wx, lru_bx,
           lru_lambda, sb_gq, sb_gk, mem_w_kv, mem_gq, mem_gk, w_pa, w_pb, w_pc, w_out,
           g_ffn, w_fc, w_down):
    for l in range(g_mix.shape[0]):
        x = _layer(x, mem, g_mix[l], g_mem[l], w_in[l], b_gate[l], conv_w[l], conv_b[l], lru_wa[l],
                   lru_ba[l], lru_wx[l], lru_bx[l], lru_lambda[l], sb_gq[l], sb_gk[l], mem_w_kv[l],
                   mem_gq[l], mem_gk[l], w_pa[l], w_pb[l], w_pc[l], w_out[l], g_ffn[l], w_fc[l],
                   w_down[l])
    return x
```

```python
import jax
import jax.numpy as jnp
from jax import lax
from jax.experimental import pallas as pl
from jax.experimental.pallas import tpu as pltpu

F32 = jnp.float32
BF16 = jnp.bfloat16

NORM_EPS = 1e-6
LRU_C = 8.0
CONV_WIDTH = 4
LRU_BLOCK = 128
SB_HEAD_DIM = 64
MEM_HEAD_DIM = 256
MEM_HEADS = 4

LANES = 128
SUBLANES = 8
MIB = 1024 * 1024

F32_MIN_NORMAL = 1.1754943508222875e-38

EXP_ZERO_BELOW = -104.0


def _params(semantics, vmem_mib):
    return pltpu.CompilerParams(dimension_semantics=semantics, vmem_limit_bytes=vmem_mib * MIB)


NORM_CHUNK = 64


def _rms_rows(x, g):
    ms = jnp.mean(x * x, axis=-1, keepdims=True)
    return x * lax.rsqrt(ms + NORM_EPS) * g


def _norm_matmul_kernel(x_ref, g_ref, w_ref, o_ref, h_ref):
    @pl.when(pl.program_id(1) == 0)
    def _():
        g = g_ref[...]

        def body(c, carry):
            rows = pl.ds(pl.multiple_of(c * NORM_CHUNK, NORM_CHUNK), NORM_CHUNK)
            h_ref[rows, :] = _rms_rows(x_ref[rows, :].astype(F32), g).astype(BF16)
            return carry

        lax.fori_loop(0, x_ref.shape[0] // NORM_CHUNK, body, 0, unroll=4)

    o_ref[...] = jnp.dot(h_ref[...], w_ref[...], preferred_element_type=F32).astype(o_ref.dtype)


def _norm_matmul(x2d, g, w, tm, tn, vmem_mib, name):
    m, k = x2d.shape
    n = w.shape[1]
    return pl.pallas_call(
        _norm_matmul_kernel,
        out_shape=jax.ShapeDtypeStruct((m, n), BF16),
        grid=(m // tm, n // tn),
        in_specs=[
            pl.BlockSpec((tm, k), lambda i, j: (i, 0)),
            pl.BlockSpec((1, k), lambda i, j: (0, 0)),
            pl.BlockSpec((k, tn), lambda i, j: (0, j)),
        ],
        out_specs=pl.BlockSpec((tm, tn), lambda i, j: (i, j)),
        scratch_shapes=[pltpu.VMEM((tm, k), BF16)],
        compiler_params=_params(("arbitrary", "arbitrary"), vmem_mib),
        name=name,
    )(x2d, g.reshape(1, k), w)


LRU_HEADS_PER_STEP = 4
SCAN_CHUNKS = SUBLANES
SCAN_PAD = SUBLANES


def _gelu_tanh(x):
    half = 0.5 * x
    inner = x * (0.7978845608028654 + (0.7978845608028654 * 0.044715) * (x * x))
    return half + half * jnp.tanh(inner)


def _sigmoid(x):
    return 0.5 + 0.5 * jnp.tanh(0.5 * x)


def _sqrt_clamped(y):
    return jnp.where(y >= F32_MIN_NORMAL, y * lax.rsqrt(y), 0.0)


def _lru_kernel(u_ref, gate_ref, cw_ref, cb_ref, w_ref, b_ref, lam_ref, o_ref, c_s, a_s, b_s, h_s, p_s):
    t_len = u_ref.shape[1]
    heads = a_s.shape[0]
    chunk = t_len // SCAN_CHUNKS
    pitch = chunk + SCAN_PAD

    for hh in range(heads):
        lanes = slice(hh * LRU_BLOCK, (hh + 1) * LRU_BLOCK)
        p_s[hh, 0:SUBLANES, :] = jnp.zeros((SUBLANES, LRU_BLOCK), F32)
        p_s[hh, SUBLANES:SUBLANES + t_len, :] = u_ref[0, :, lanes].astype(F32)
        c = cb_ref[:, lanes]
        for k in range(CONV_WIDTH):
            start = SUBLANES - (CONV_WIDTH - 1) + k
            c = c + p_s[hh, start:start + t_len, :] * cw_ref[k:k + 1, lanes]
        c_s[:, lanes] = c

    lam = lam_ref[...]
    half_neg_sp = (-0.5 * LRU_C) * (jnp.maximum(-lam, 0.0) + jnp.log(1.0 + jnp.exp(-jnp.abs(lam))))

    def gates(ci, carry):
        rows = pl.ds(pl.multiple_of(ci * chunk, chunk), chunk)
        dst = pl.ds(pl.multiple_of(ci * pitch, SUBLANES), chunk)
        first = jnp.logical_and(lax.broadcasted_iota(jnp.int32, (chunk, 1), 0) == 0, ci == 0)
        for hh in range(heads):
            lanes = slice(hh * LRU_BLOCK, (hh + 1) * LRU_BLOCK)
            cc = c_s[rows, lanes]
            t = jnp.tanh(jnp.dot(cc.astype(BF16), w_ref[hh], preferred_element_type=F32) + b_ref[hh])
            half_sp = half_neg_sp[:, lanes]
            log_a = half_sp * t[:, :LRU_BLOCK] + half_sp
            half_c = 0.5 * cc
            gated_c = half_c * t[:, LRU_BLOCK:] + half_c
            a = jnp.exp(log_a)
            mult = _sqrt_clamped(-jnp.tanh(log_a) * (a * a + 1.0))
            mult = jnp.where(first, 1.0, mult)
            a_s[hh, dst, :] = a
            b_s[hh, dst, :] = mult * gated_c
        return carry

    lax.fori_loop(0, SCAN_CHUNKS, gates, 0)

    def scan(j, state):
        src = pl.ds(j, SCAN_CHUNKS, stride=pitch)
        dst = pl.ds(pl.multiple_of(j * SCAN_CHUNKS, SCAN_CHUNKS), SCAN_CHUNKS)
        new = []
        for hh in range(heads):
            h, p = state[hh]
            a = a_s[hh, src, :]
            h = a * h + b_s[hh, src, :]
            p = a * p
            h_s[hh, dst, :] = h
            p_s[hh, dst, :] = p
            new.append((h, p))
        return tuple(new)

    zeros = jnp.zeros((SCAN_CHUNKS, LRU_BLOCK), F32)
    ends = lax.fori_loop(0, chunk, scan, tuple((zeros, zeros + 1.0) for _ in range(heads)), unroll=8)

    h_in = []
    for hh in range(heads):
        h_end, p_end = ends[hh]
        rows_in = [jnp.zeros((1, LRU_BLOCK), F32)]
        for s in range(1, SCAN_CHUNKS):
            rows_in.append(h_end[s - 1:s, :] + p_end[s - 1:s, :] * rows_in[-1])
        h_in.append(jnp.concatenate(rows_in, axis=0))

    def patch(j, carry):
        src = pl.ds(pl.multiple_of(j * SCAN_CHUNKS, SCAN_CHUNKS), SCAN_CHUNKS)
        dst = pl.ds(j, SCAN_CHUNKS, stride=pitch)
        for hh in range(heads):
            b_s[hh, dst, :] = h_s[hh, src, :] + p_s[hh, src, :] * h_in[hh]
        return carry

    lax.fori_loop(0, chunk, patch, 0, unroll=8)

    def emit(ci, carry):
        rows = pl.ds(pl.multiple_of(ci * chunk, chunk), chunk)
        src = pl.ds(pl.multiple_of(ci * pitch, SUBLANES), chunk)
        for hh in range(heads):
            lanes = slice(hh * LRU_BLOCK, (hh + 1) * LRU_BLOCK)
            gate = gate_ref[0, rows, lanes].astype(F32)
            o_ref[0, rows, lanes] = (b_s[hh, src, :] * _gelu_tanh(gate)).astype(o_ref.dtype)
        return carry

    lax.fori_loop(0, SCAN_CHUNKS, emit, 0)


def _lru(proj3, conv_w, conv_b, w_cat, b_cat, lam, width):
    bsz, t_len, _ = proj3.shape
    hps = LRU_HEADS_PER_STEP
    lw = hps * LRU_BLOCK
    groups = width // lw
    scan_rows = SCAN_CHUNKS * (t_len // SCAN_CHUNKS + SCAN_PAD)
    return pl.pallas_call(
        _lru_kernel,
        out_shape=jax.ShapeDtypeStruct((bsz, t_len, width), BF16),
        grid=(bsz, groups),
        in_specs=[
            pl.BlockSpec((1, t_len, lw), lambda b, h: (b, 0, h)),
            pl.BlockSpec((1, t_len, lw), lambda b, h: (b, 0, groups + h)),
            pl.BlockSpec((CONV_WIDTH, lw), lambda b, h: (0, h)),
            pl.BlockSpec((1, lw), lambda b, h: (0, h)),
            pl.BlockSpec((hps, LRU_BLOCK, 2 * LRU_BLOCK), lambda b, h: (h, 0, 0)),
            pl.BlockSpec((hps, 1, 2 * LRU_BLOCK), lambda b, h: (h, 0, 0)),
            pl.BlockSpec((1, lw), lambda b, h: (0, h)),
        ],
        out_specs=pl.BlockSpec((1, t_len, lw), lambda b, h: (b, 0, h)),
        scratch_shapes=[pltpu.VMEM((t_len, lw), F32),
                        pltpu.VMEM((hps, scan_rows, LRU_BLOCK), F32),
                        pltpu.VMEM((hps, scan_rows, LRU_BLOCK), F32),
                        pltpu.VMEM((hps, t_len, LRU_BLOCK), F32),
                        pltpu.VMEM((hps, t_len + SUBLANES, LRU_BLOCK), F32)],
        compiler_params=_params(("arbitrary", "arbitrary"), 48),
        name="lru",
    )(proj3, proj3, conv_w, conv_b.reshape(1, width), w_cat, b_cat, lam.reshape(1, width))


SB_TILE = 256
SB_TILES_PER_STEP = 8
SB_BLOCKS_AHEAD = 2


def _pair_rms(x, g, seg):
    x2 = x * x
    hi = x2.astype(BF16)
    lo = (x2 - hi.astype(F32)).astype(BF16)
    ss = jnp.dot(jnp.concatenate([hi, lo], axis=1), seg, preferred_element_type=F32)
    return x * lax.rsqrt(ss * (1.0 / SB_HEAD_DIM) + NORM_EPS) * g


def _sb_kernel(q_ref, k_ref, v_ref, gq_ref, gk_ref, tri_ref, seg_ref, o_ref,
               k2_s, v2_s, qs_s, acc_s, car_s, live_s):
    t_len = k_ref.shape[1]
    qi = pl.program_id(2)
    first_head = lax.broadcasted_iota(jnp.int32, (1, LANES), 1) < SB_HEAD_DIM
    zero = jnp.zeros((), BF16)

    @pl.when(qi == 0)
    def _():
        gk = gk_ref[...]

        def body(c, carry):
            rows = pl.ds(pl.multiple_of(c * SB_TILE, SB_TILE), SB_TILE)
            dst0 = pl.ds(pl.multiple_of(c * 2 * SB_TILE, SB_TILE), SB_TILE)
            dst1 = pl.ds(pl.multiple_of(c * 2 * SB_TILE + SB_TILE, SB_TILE), SB_TILE)
            kn = _pair_rms(k_ref[0, rows, :].astype(F32), gk, seg_ref[...]).astype(BF16)
            vb = v_ref[0, rows, :]
            k2_s[dst0, :] = jnp.where(first_head, kn, zero)
            k2_s[dst1, :] = jnp.where(first_head, zero, kn)
            v2_s[dst0, :] = jnp.where(first_head, vb, zero)
            v2_s[dst1, :] = jnp.where(first_head, zero, vb)
            return carry

        lax.fori_loop(0, t_len // SB_TILE, body, 0, unroll=4)

    scale = SB_HEAD_DIM ** -0.5
    qs_s[...] = _pair_rms(q_ref[0].astype(F32), gq_ref[...], seg_ref[...]).astype(BF16) * jnp.asarray(scale, BF16)
    sign_bit = jnp.uint32(0x80000000)
    first_tile = qi * SB_TILES_PER_STEP

    def tile_rows(t):
        return slice(t * SB_TILE, (t + 1) * SB_TILE)

    def block_rows(kb):
        return pl.ds(pl.multiple_of(kb * 2 * SB_TILE, 2 * SB_TILE), 2 * SB_TILE)

    def causal_mask():
        return (lax.broadcasted_iota(jnp.int32, (SB_TILE, SB_TILE), 1)
                < lax.broadcasted_iota(jnp.int32, (SB_TILE, SB_TILE), 0))

    def logits(t, kb, causal):
        z = lax.dot_general(qs_s[tile_rows(t), :], k2_s[block_rows(kb), :], (((1,), (1,)), ((), ())),
                            preferred_element_type=F32)
        out = []
        for h in range(2):
            zh = z[:, h * SB_TILE:(h + 1) * SB_TILE]
            neg_abs = lax.bitcast_convert_type(lax.bitcast_convert_type(zh, jnp.uint32) | sign_bit, F32)
            ls = jnp.minimum(zh, 0.0) - jnp.log(1.0 + jnp.exp(neg_abs))
            l1m = ls - zh
            if causal is not None:
                l1m = jnp.where(causal, l1m, 0.0)
            out.append((ls, l1m))
        return out

    def weights(ls_l1m, causal):
        ws, sums = [], []
        for ls, l1m in ls_l1m:
            tail = jnp.dot(l1m.astype(BF16), tri_ref[...], preferred_element_type=F32)
            w = jnp.exp(ls + tail)
            if causal is not None:
                w = jnp.where(causal, w, 0.0)
            ws.append(w.astype(BF16))
            sums.append(tail[:, 0:1] + l1m[:, 0:1])
        return jnp.concatenate(ws, axis=1), sums

    def weighted_values(kb, w):
        return jnp.dot(w, v2_s[block_rows(kb), :], preferred_element_type=F32)

    def later_factor(carry0, carry1):
        return jnp.where(first_head, jnp.exp(carry0), jnp.exp(carry1))

    def still_live(carry0, carry1):
        return (jnp.max(jnp.maximum(carry0, carry1)) > EXP_ZERO_BELOW).astype(jnp.int32)

    def region(units):
        causal = causal_mask()
        lg = [[logits(t, kb, causal if d else None) for kb, d in blocks] for t, blocks in enumerate(units)]
        wt = [[weights(l, causal if d else None) for l, (_, d) in zip(lg[t], blocks)]
              for t, blocks in enumerate(units)]
        for t, blocks in enumerate(units):
            acc = carry0 = carry1 = None
            for (kb, _), (w, sums) in zip(blocks, wt[t]):
                pv = weighted_values(kb, w)
                if acc is None:
                    acc, carry0, carry1 = pv, sums[0], sums[1]
                else:
                    acc = acc + later_factor(carry0, carry1) * pv
                    carry0, carry1 = carry0 + sums[0], carry1 + sums[1]
            acc_s[tile_rows(t), :] = acc
            car_s[2 * t] = carry0
            car_s[2 * t + 1] = carry1
            live_s[t] = still_live(carry0, carry1)

    @pl.when(qi == 0)
    def _():
        region([[(first_tile + t - j, j == 0) for j in range(min(t + 1, SB_BLOCKS_AHEAD))]
                for t in range(SB_TILES_PER_STEP)])

    @pl.when(qi != 0)
    def _():
        region([[(first_tile + t - j, j == 0) for j in range(SB_BLOCKS_AHEAD)]
                for t in range(SB_TILES_PER_STEP)])

    def cond(state):
        kb, live = state
        return jnp.logical_and(kb >= 0, live > 0)

    for t in range(SB_TILES_PER_STEP):
        def body(state, t=t):
            kb, _ = state
            carry0, carry1 = car_s[2 * t], car_s[2 * t + 1]
            w, sums = weights(logits(t, kb, None), None)
            acc_s[tile_rows(t), :] += later_factor(carry0, carry1) * weighted_values(kb, w)
            carry0, carry1 = carry0 + sums[0], carry1 + sums[1]
            car_s[2 * t] = carry0
            car_s[2 * t + 1] = carry1
            return kb - 1, still_live(carry0, carry1)

        lax.while_loop(cond, body, (first_tile + t - SB_BLOCKS_AHEAD, live_s[t]))

    o_ref[0] = acc_s[...].astype(o_ref.dtype)


def _sb_attention(proj3, gq2, gk2, q_col, k_col, v_col, heads):
    bsz, t_len, _ = proj3.shape
    pairs = heads // 2
    lane_head = jnp.arange(LANES) // SB_HEAD_DIM
    same_head = (lane_head[:, None] == lane_head[None, :]).astype(BF16)
    seg2 = jnp.concatenate([same_head, same_head], axis=0)
    key = jnp.arange(SB_TILE)
    later_key = (key[:, None] > key[None, :]).astype(BF16)
    q_rows = SB_TILE * SB_TILES_PER_STEP
    return pl.pallas_call(
        _sb_kernel,
        out_shape=jax.ShapeDtypeStruct((bsz, t_len, heads * SB_HEAD_DIM), BF16),
        grid=(bsz, pairs, t_len // q_rows),
        in_specs=[
            pl.BlockSpec((1, q_rows, LANES), lambda b, p, i: (b, i, q_col // LANES + p)),
            pl.BlockSpec((1, t_len, LANES), lambda b, p, i: (b, 0, k_col // LANES + p)),
            pl.BlockSpec((1, t_len, LANES), lambda b, p, i: (b, 0, v_col // LANES + p)),
            pl.BlockSpec((1, LANES), lambda b, p, i: (0, 0)),
            pl.BlockSpec((1, LANES), lambda b, p, i: (0, 0)),
            pl.BlockSpec((SB_TILE, SB_TILE), lambda b, p, i: (0, 0)),
            pl.BlockSpec((2 * LANES, LANES), lambda b, p, i: (0, 0)),
        ],
        out_specs=pl.BlockSpec((1, q_rows, LANES), lambda b, p, i: (b, i, p)),
        scratch_shapes=[pltpu.VMEM((2 * t_len, LANES), BF16),
                        pltpu.VMEM((2 * t_len, LANES), BF16),
                        pltpu.VMEM((q_rows, LANES), BF16),
                        pltpu.VMEM((q_rows, LANES), F32),
                        pltpu.VMEM((2 * SB_TILES_PER_STEP, SB_TILE, 1), F32),
                        pltpu.SMEM((SB_TILES_PER_STEP,), jnp.int32)],
        compiler_params=_params(("arbitrary", "arbitrary", "arbitrary"), 32),
        name="sb_attn",
    )(proj3, proj3, proj3, gq2, gk2, later_key, seg2)


MEM_TILE = 512


def _mem_attn_kernel(q_ref, kv_ref, gq_ref, gk_ref, o_ref, kn_s):
    width = MEM_HEADS * MEM_HEAD_DIM

    @pl.when(pl.program_id(1) == 0)
    def _():
        for h in range(MEM_HEADS):
            cols = slice(h * MEM_HEAD_DIM, (h + 1) * MEM_HEAD_DIM)
            kn_s[:, cols] = _rms_rows(kv_ref[0, :, cols].astype(F32), gk_ref[...]).astype(BF16)

    scale = MEM_HEAD_DIM ** -0.5
    for h in range(MEM_HEADS):
        cols = slice(h * MEM_HEAD_DIM, (h + 1) * MEM_HEAD_DIM)
        qn = _rms_rows(q_ref[0, :, cols].astype(F32), gq_ref[...]).astype(BF16) * jnp.asarray(scale, BF16)
        s = lax.dot_general(qn, kn_s[:, cols], (((1,), (1,)), ((), ())), preferred_element_type=F32)
        e = jnp.exp(s - jnp.max(s, axis=-1, keepdims=True))
        denom = jnp.sum(e, axis=-1, keepdims=True)
        v = kv_ref[0, :, width + h * MEM_HEAD_DIM:width + (h + 1) * MEM_HEAD_DIM]
        o = jnp.dot(e.astype(BF16), v, preferred_element_type=F32) / denom
        o_ref[0, :, cols] = o.astype(o_ref.dtype)


def _mem_attention(proj3, memkv3, gq, gk, q_col):
    bsz, t_len, _ = proj3.shape
    m_len = memkv3.shape[1]
    width = MEM_HEADS * MEM_HEAD_DIM
    return pl.pallas_call(
        _mem_attn_kernel,
        out_shape=jax.ShapeDtypeStruct((bsz, t_len, width), BF16),
        grid=(bsz, t_len // MEM_TILE),
        in_specs=[
            pl.BlockSpec((1, MEM_TILE, width), lambda b, i: (b, i, q_col // width)),
            pl.BlockSpec((1, m_len, 2 * width), lambda b, i: (b, 0, 0)),
            pl.BlockSpec((1, MEM_HEAD_DIM), lambda b, i: (0, 0)),
            pl.BlockSpec((1, MEM_HEAD_DIM), lambda b, i: (0, 0)),
        ],
        out_specs=pl.BlockSpec((1, MEM_TILE, width), lambda b, i: (b, i, 0)),
        scratch_shapes=[pltpu.VMEM((m_len, width), BF16)],
        compiler_params=_params(("arbitrary", "arbitrary"), 32),
        name="mem_attn",
    )(proj3, memkv3, gq.reshape(1, MEM_HEAD_DIM), gk.reshape(1, MEM_HEAD_DIM))


MERGE_TM = 1024
MERGE_TN = 512


def _merge_kernel(ya_ref, yb_ref, yc_ref, wa_ref, wb_ref, wc_ref, g0_ref, g1_ref, g2_ref,
                  b0_ref, b1_ref, b2_ref, o_ref):
    tn = o_ref.shape[1]
    cols = pl.ds(pl.multiple_of(pl.program_id(1) * tn, tn), tn)
    out = None
    for y_ref, w_ref, g_ref, b_ref in ((ya_ref, wa_ref, g0_ref, b0_ref), (yb_ref, wb_ref, g1_ref, b1_ref),
                                       (yc_ref, wc_ref, g2_ref, b2_ref)):
        gate = _sigmoid(g_ref[...].astype(F32) + b_ref[...])
        term = gate * jnp.dot(y_ref[...], w_ref[:, cols], preferred_element_type=F32)
        out = term if out is None else out + term
    o_ref[...] = out.astype(o_ref.dtype)


def _merge(ya, yb, yc, w_pa, w_pb, w_pc, proj2, b_gate, gate_col, d_model):
    n = ya.shape[0]
    tm, tn = MERGE_TM, MERGE_TN
    gate_blk = gate_col // tn
    per_branch = d_model // tn

    def y_spec(width):
        return pl.BlockSpec((tm, width), lambda i, j: (i, 0))

    def w_spec(width):
        return pl.BlockSpec((width, d_model), lambda i, j: (0, 0), pipeline_mode=pl.Buffered(1))

    def g_spec(branch):
        return pl.BlockSpec((tm, tn), lambda i, j: (i, gate_blk + branch * per_branch + j))

    def b_spec(branch):
        return pl.BlockSpec((1, tn), lambda i, j: (0, branch * per_branch + j))

    b_gate2 = b_gate.reshape(1, -1)
    return pl.pallas_call(
        _merge_kernel,
        out_shape=jax.ShapeDtypeStruct((n, d_model), BF16),
        grid=(n // tm, d_model // tn),
        in_specs=[y_spec(ya.shape[1]), y_spec(yb.shape[1]), y_spec(yc.shape[1]),
                  w_spec(w_pa.shape[0]), w_spec(w_pb.shape[0]), w_spec(w_pc.shape[0]),
                  g_spec(0), g_spec(1), g_spec(2), b_spec(0), b_spec(1), b_spec(2)],
        out_specs=pl.BlockSpec((tm, tn), lambda i, j: (i, j)),
        compiler_params=_params(("arbitrary", "arbitrary"), 48),
        name="merge",
    )(ya, yb, yc, w_pa, w_pb, w_pc, proj2, proj2, proj2, b_gate2, b_gate2, b_gate2)


OUT_TM = 512


def _out_proj_kernel(x_ref, m_ref, w_ref, x1_ref):
    x1_ref[...] = x_ref[...] + jnp.dot(m_ref[...], w_ref[...], preferred_element_type=F32)


def _out_proj(x2, merged, w_out):
    n, d = x2.shape
    tm = OUT_TM
    row = pl.BlockSpec((tm, d), lambda i: (i, 0))
    return pl.pallas_call(
        _out_proj_kernel,
        out_shape=jax.ShapeDtypeStruct((n, d), F32),
        grid=(n // tm,),
        in_specs=[row, row, pl.BlockSpec((d, d), lambda i: (0, 0))],
        out_specs=row,
        compiler_params=_params(("arbitrary",), 40),
        name="out_proj",
    )(x2, merged, w_out)


FFN_TM = 1024
FFN_TF = 512
FFN_TN = 512
FFN_NORM_ROWS = 112


def _ffn_up_kernel(x1_ref, g_ref, wg_ref, wu_ref, o_ref, h_s):
    i, f = pl.program_id(0), pl.program_id(1)
    tm = x1_ref.shape[0]
    slot = i % 2

    @pl.when(jnp.logical_and(i == 0, f == 0))
    def _():
        def body(c, carry):
            rows = pl.ds(pl.multiple_of(c * NORM_CHUNK, NORM_CHUNK), NORM_CHUNK)
            h_s[0, rows, :] = _rms_rows(x1_ref[rows, :], g_ref[...]).astype(BF16)
            return carry

        lax.fori_loop(0, tm // NORM_CHUNK, body, 0, unroll=2)

    h = h_s[slot]
    gate = jnp.dot(h, wg_ref[...], preferred_element_type=F32)
    up = jnp.dot(h, wu_ref[...], preferred_element_type=F32)
    o_ref[...] = (gate * _sigmoid(gate) * up).astype(o_ref.dtype)

    start = jnp.clip((f - 1) * FFN_NORM_ROWS, 0, tm - FFN_NORM_ROWS)
    rows = pl.ds(pl.multiple_of(start, 2 * SUBLANES), FFN_NORM_ROWS)
    h_s[1 - slot, rows, :] = _rms_rows(x1_ref[rows, :], g_ref[...]).astype(BF16)


def _ffn_down_kernel(a_ref, w_ref, x1_ref, o_ref):
    tn = o_ref.shape[1]
    cols = pl.ds(pl.multiple_of(pl.program_id(1) * tn, tn), tn)
    o_ref[...] = x1_ref[...] + jnp.dot(a_ref[...], w_ref[:, cols], preferred_element_type=F32)


def _ffn(x1, g_ffn, w_fc, w_down):
    n, d = x1.shape
    d_ff = w_down.shape[0]
    tm, tf, tn = FFN_TM, FFN_TF, FFN_TN
    n_f = d_ff // tf
    last = n // tm - 1
    assert (n_f - 1) * FFN_NORM_ROWS >= tm and FFN_NORM_ROWS % (2 * SUBLANES) == 0 and tm % (2 * SUBLANES) == 0

    def row_tile_to_normalise(i, f):
        return jnp.where(jnp.logical_and(i == 0, f == 0), 0, jnp.minimum(i + 1, last)), 0

    act = pl.pallas_call(
        _ffn_up_kernel,
        out_shape=jax.ShapeDtypeStruct((n, d_ff), BF16),
        grid=(n // tm, n_f),
        in_specs=[pl.BlockSpec((tm, d), row_tile_to_normalise),
                  pl.BlockSpec((1, d), lambda i, f: (0, 0)),
                  pl.BlockSpec((d, tf), lambda i, f: (0, f)),
                  pl.BlockSpec((d, tf), lambda i, f: (0, n_f + f))],
        out_specs=pl.BlockSpec((tm, tf), lambda i, f: (i, f)),
        scratch_shapes=[pltpu.VMEM((2, tm, d), BF16)],
        compiler_params=_params(("arbitrary", "arbitrary"), 48),
        name="ffn_up",
    )(x1, g_ffn.reshape(1, d), w_fc, w_fc)
    return pl.pallas_call(
        _ffn_down_kernel,
        out_shape=jax.ShapeDtypeStruct((n, d), F32),
        grid=(n // tm, d // tn),
        in_specs=[pl.BlockSpec((tm, d_ff), lambda i, j: (i, 0)),
                  pl.BlockSpec((d_ff, d), lambda i, j: (0, 0), pipeline_mode=pl.Buffered(1)),
                  pl.BlockSpec((tm, tn), lambda i, j: (i, j))],
        out_specs=pl.BlockSpec((tm, tn), lambda i, j: (i, j)),
        compiler_params=_params(("arbitrary", "arbitrary"), 60),
        name="ffn_down",
    )(act, w_down, x1)


def _layer(x, mem, g_mix, g_mem, w_in, b_gate, conv_w, conv_b, lru_wa, lru_ba, lru_wx, lru_bx,
           lru_lambda, sb_gq, sb_gk, mem_w_kv, mem_gq, mem_gk, w_pa, w_pb, w_pc, w_out,
           g_ffn, w_fc, w_down):
    bsz, t_len, d = x.shape
    m_len = mem.shape[1]
    n = bsz * t_len
    lru_width = w_pa.shape[0]
    sb_width = w_pb.shape[0]
    mem_width = w_pc.shape[0]
    in_cols = w_in.shape[1]
    q_sb_col = 2 * lru_width
    k_sb_col = q_sb_col + sb_width
    v_sb_col = k_sb_col + sb_width
    q_mem_col = v_sb_col + sb_width
    gate_col = q_mem_col + mem_width

    x2 = x.reshape(n, d)
    proj2 = _norm_matmul(x2, g_mix, w_in.astype(BF16), 1024, 2048, 56, "in_proj")
    proj3 = proj2.reshape(bsz, t_len, in_cols)

    w_cat = (0.5 * jnp.concatenate([lru_wa, lru_wx], axis=-1)).astype(BF16)
    b_cat = 0.5 * jnp.concatenate([lru_ba, lru_bx], axis=-1)[:, None, :]
    ya = _lru(proj3, conv_w, conv_b, w_cat, b_cat, lru_lambda, lru_width)

    yb = _sb_attention(proj3, jnp.tile(sb_gq, 2)[None, :], jnp.tile(sb_gk, 2)[None, :],
                       q_sb_col, k_sb_col, v_sb_col, sb_width // SB_HEAD_DIM)

    memkv = _norm_matmul(mem.reshape(bsz * m_len, d), g_mem, mem_w_kv.astype(BF16), 1024, 1024, 48, "mem_kv")
    yc = _mem_attention(proj3, memkv.reshape(bsz, m_len, 2 * mem_width), mem_gq, mem_gk, q_mem_col)

    merged = _merge(ya.reshape(n, lru_width), yb.reshape(n, sb_width), yc.reshape(n, mem_width),
                    w_pa.astype(BF16), w_pb.astype(BF16), w_pc.astype(BF16), proj2, b_gate, gate_col, d)
    x1 = _out_proj(x2, merged, w_out.astype(BF16))
    out = _ffn(x1, g_ffn, w_fc.astype(BF16), w_down.astype(BF16))
    return out.reshape(bsz, t_len, d)


def kernel(x, mem, g_mix, g_mem, w_in, b_gate, conv_w, conv_b, lru_wa, lru_ba, lru_wx, lru_bx,
           lru_lambda, sb_gq, sb_gk, mem_w_kv, mem_gq, mem_gk, w_pa, w_pb, w_pc, w_out,
           g_ffn, w_fc, w_down):
    for l in range(g_mix.shape[0]):
        x = _layer(x, mem, g_mix[l], g_mem[l], w_in[l], b_gate[l], conv_w[l], conv_b[l], lru_wa[l],
                   lru_ba[l], lru_wx[l], lru_bx[l], lru_lambda[l], sb_gq[l], sb_gk[l], mem_w_kv[l],
                   mem_gq[l], mem_gk[l], w_pa[l], w_pb[l], w_pc[l], w_out[l], g_ffn[l], w_fc[l],
                   w_down[l])
    return x
```
